```python
import math
import jax
import jax.numpy as jnp
from jax import lax
import numpy as np

D_MODEL = 2048
BATCH = 32
SEQ = 256
DEPTH = 2
DEC_BATCH = 2
DEC_SEQ = 4096
PAST_LEN = 512

GRID_W = 64
H_A = 6
DH_A = 64
DV_A = 2 * DH_A
H_B = 6
DK_B = 128
DV_B = 128
SHORT_CONV = 3
CHUNK = 64
N_POOL = 4
POOL_CH = 128
POOL_WINDOWS = (2, 4, 8, 16)
W_A = H_A * DV_A
W_B = H_B * DV_B
W_C = N_POOL * POOL_CH
MIX_W = W_A + W_B + W_C
IN_SIZES = (H_A * 2 * DH_A, H_A * 2 * DH_A, H_A * DV_A, H_B * (2 * DK_B + DV_B), H_B * DV_B, 2 * H_B, 2 * H_B, W_C)
IN_W = sum(IN_SIZES)
D_FF = 5632
FFN_CONV = 3
ROPE_THETA = 10000.0
Q_BLOCK = 128
EPS = 1e-6

kernel_name = 'hybrid_diffusion_prefix_step'


def rmsnorm(x, g):
    xf = x.astype(jnp.float32)
    y = xf * lax.rsqrt(jnp.mean(xf * xf, axis=-1, keepdims=True) + EPS) * g.astype(jnp.float32)
    return y.astype(x.dtype)


def l2norm(x):
    xf = x.astype(jnp.float32)
    return xf * lax.rsqrt(jnp.sum(xf * xf, axis=-1, keepdims=True) + EPS)


def dwconv_centred(x, w):
    k_w = w.shape[0]
    pad = k_w // 2
    n = x.shape[1]
    xp = jnp.pad(x, ((0, 0), (pad, pad), (0, 0)))
    return sum(xp[:, j:j + n] * w[j] for j in range(k_w))


def axial_rope(x):
    n = x.shape[1]
    rows = n // GRID_W
    row = jnp.repeat(jnp.arange(rows), GRID_W)
    col = jnp.arange(rows * GRID_W) % GRID_W
    n_pair = DH_A // 4
    inv = ROPE_THETA ** (-jnp.arange(n_pair, dtype=jnp.float32) / n_pair)
    ang = jnp.concatenate([row[:, None] * inv, col[:, None] * inv], axis=-1)
    cos = jnp.cos(ang)[None, :, None, None, :]
    sin = jnp.sin(ang)[None, :, None, None, :]
    xp = x.astype(jnp.float32).reshape(x.shape[:-1] + (DH_A // 2, 2))
    x1, x2 = xp[..., 0], xp[..., 1]
    out = jnp.stack([x1 * cos - x2 * sin, x1 * sin + x2 * cos], axis=-1)
    return out.reshape(x.shape).astype(x.dtype)


def diff_attention(q, k, v, lam):
    b, n = q.shape[:2]
    nb = n // Q_BLOCK
    qb = jnp.moveaxis(q.reshape(b, nb, Q_BLOCK, H_A, 2, DH_A), 1, 0)

    def block(qblk):
        s = jnp.einsum('bqhmd,bkhmd->bmhqk', qblk, k).astype(jnp.float32) * (DH_A ** -0.5)
        p = jax.nn.softmax(s, axis=-1)
        p = p[:, 0] - lam * p[:, 1]
        return jnp.einsum('bhqk,bkhv->bqhv', p.astype(v.dtype), v)

    o = lax.map(block, qb)
    return jnp.moveaxis(o, 0, 1).reshape(b, n, H_A, DV_A)


def gated_delta_chunked(q, k, v, g, beta, s0):
    b, n, h, dk = k.shape
    dv = v.shape[-1]
    nc = n // CHUNK
    f32 = jnp.float32

    def chunks(t):
        return t.astype(f32).reshape(b, nc, CHUNK, h, -1).transpose(1, 0, 3, 2, 4)

    qc = chunks(l2norm(q)) * (dk ** -0.5)
    kc = chunks(l2norm(k))
    vc = chunks(v)
    bc = chunks(beta[..., None])
    gc = jnp.cumsum(chunks(g[..., None])[..., 0], axis=-1)
    idx = jnp.arange(CHUNK)
    incl = idx[:, None] >= idx[None, :]
    strict = idx[:, None] > idx[None, :]
    diff = gc[..., :, None] - gc[..., None, :]
    decay = jnp.where(incl, jnp.exp(jnp.where(incl, diff, 0.0)), 0.0)
    kb = kc * bc
    lmat = jnp.where(strict, jnp.einsum('nbhid,nbhjd->nbhij', kb, kc) * decay, 0.0)
    eye = jnp.eye(CHUNK, dtype=f32)
    tinv = lax.linalg.triangular_solve(eye + lmat, jnp.broadcast_to(eye, lmat.shape),
                                       left_side=True, lower=True, unit_diagonal=True)
    u = tinv @ (vc * bc)
    w = tinv @ (kb * jnp.exp(gc)[..., None])
    a_intra = jnp.einsum('nbhid,nbhjd->nbhij', qc, kc) * decay
    q_dec = qc * jnp.exp(gc)[..., None]
    k_dec = kc * jnp.exp(gc[..., -1:] - gc)[..., None]
    g_last = jnp.exp(gc[..., -1])

    def step(s, xs):
        q_i, k_i, u_i, w_i, a_i, gl_i = xs
        v_new = u_i - w_i @ s
        o_i = q_i @ s + a_i @ v_new
        s = s * gl_i[..., None, None] + jnp.einsum('bhck,bhcv->bhkv', k_i, v_new)
        return s, o_i

    s_fin, o = lax.scan(step, s0.astype(f32), (q_dec, k_dec, u, w, a_intra, g_last))
    o = o.transpose(1, 0, 3, 2, 4).reshape(b, n, h, dv)
    return o.astype(v.dtype), s_fin


def pool_mixer(u, w_pool, scale):
    b, n, _ = u.shape
    ug = u.reshape(b, n, N_POOL, POOL_CH)
    cs = jnp.pad(jnp.cumsum(ug.astype(jnp.float32), axis=1), ((0, 0), (1, 0), (0, 0), (0, 0)))
    win = jnp.array(POOL_WINDOWS)
    t = jnp.arange(n)[:, None]
    lo = jnp.clip(t - win // 2, 0, n)
    hi = jnp.clip(t - win // 2 + win, 0, n)
    gidx = jnp.arange(N_POOL)[None, :]
    mean = (cs[:, hi, gidx] - cs[:, lo, gidx]) / (hi - lo).astype(jnp.float32)[None, :, :, None]
    y = (mean - ug.astype(jnp.float32)).astype(u.dtype)
    y = jnp.einsum('bngc,gcd->bngd', y, w_pool)
    return y.reshape(b, n, W_C) * scale


def conv_ffn(h, w_up, w_conv, w_down):
    a = dwconv_centred(h @ w_up, w_conv)
    gate, up = jnp.split(a, 2, axis=-1)
    return (jax.nn.silu(gate) * up) @ w_down


def layer(x, cond, p, l, ctx):
    b, n, _ = x.shape
    mod = jax.nn.silu(cond) @ p['w_mod'][l] + p['b_mod'][l]
    sh1, sc1, gt1, sh2, sc2, gt2 = jnp.split(mod, 6, axis=-1)
    gains = p['norm_gains'][l]
    h = rmsnorm(x, gains[0]) * (1 + sc1) + sh1
    z = h @ p['w_in'][l]
    offs = [int(o) for o in np.cumsum(IN_SIZES)[:-1]]
    qa, ka, va, qkvb, zb, beta_r, alpha_r, pu = jnp.split(z, offs, axis=-1)

    qa = qa.reshape(b, n, H_A, 2, DH_A)
    ka = ka.reshape(b, n, H_A, 2, DH_A)
    va = va.reshape(b, n, H_A, DV_A)
    lam_init = 0.8 - 0.6 * math.exp(-0.3 * l)
    lq = p['lambda_qk'][l].astype(jnp.float32)
    lam = jnp.exp(jnp.sum(lq[0] * lq[1])) - jnp.exp(jnp.sum(lq[2] * lq[3])) + lam_init
    if ctx is None:
        q_use, k_all, v_all = qa, ka, va
    else:
        k_ctx, v_ctx, s_ctx = ctx
        q_use = axial_rope(qa)
        k_all = jnp.concatenate([axial_rope(ka), k_ctx.reshape(b, -1, H_A, 2, DH_A)], axis=1)
        v_all = jnp.concatenate([va, v_ctx], axis=1)
    oa = diff_attention(q_use, k_all, v_all, lam)
    oa = rmsnorm(oa, p['g_diff'][l]) * (1 - lam_init)

    qkvb = jax.nn.silu(dwconv_centred(qkvb, p['conv_qkv'][l]))
    qb, kb, vb = jnp.split(qkvb, [H_B * DK_B, 2 * H_B * DK_B], axis=-1)
    qb = qb.reshape(b, n, H_B, DK_B)
    kb = kb.reshape(b, n, H_B, DK_B)
    vb = vb.reshape(b, n, H_B, DV_B)
    beta = jax.nn.sigmoid(beta_r.astype(jnp.float32)).reshape(b, n, 2, H_B)
    gdec = -jnp.exp(p['a_log'][l].astype(jnp.float32)) * jax.nn.softplus(
        alpha_r.astype(jnp.float32).reshape(b, n, 2, H_B) + p['dt_bias'][l].astype(jnp.float32))
    s0 = jnp.zeros((b, 2, H_B, DK_B, DV_B), x.dtype) if ctx is None else s_ctx
    o_f, s_f = gated_delta_chunked(qb, kb, vb, gdec[:, :, 0], beta[:, :, 0], s0[:, 0])
    flip = lambda t: jnp.flip(t, axis=1)
    o_b, s_b = gated_delta_chunked(flip(qb), flip(kb), flip(vb), flip(gdec[:, :, 1]), flip(beta[:, :, 1]), s0[:, 1])
    ob = o_f + flip(o_b)
    ob = rmsnorm(ob, p['g_delta'][l]) * jax.nn.silu(zb.reshape(b, n, H_B, DV_B))

    oc = pool_mixer(pu, p['w_pool'][l], p['pool_scale'][l])

    mix = jnp.concatenate([oa.reshape(b, n, W_A), ob.reshape(b, n, W_B), oc], axis=-1) @ p['w_out'][l]
    x = x + gt1 * rmsnorm(mix, gains[1])
    h2 = rmsnorm(x, gains[2]) * (1 + sc2) + sh2
    f = conv_ffn(h2, p['w_up'][l], p['conv_ffn'][l], p['w_down'][l])
    x = x + gt2 * rmsnorm(f, gains[3])
    if ctx is None:
        return x, (ka.reshape(b, n, H_A, 2 * DH_A), va, jnp.stack([s_f, s_b], axis=1).astype(x.dtype))
    return x, None


def setup_inputs(seed: int = 0) -> dict:
    key = jax.random.key(seed)
    ks = jax.random.split(key, 24)
    nrm = jax.random.normal
    f32 = jnp.float32
    u_dt = jax.random.uniform(ks[15], (DEPTH, 2, H_B), dtype=f32)
    dt = jnp.exp(u_dt * (math.log(0.1) - math.log(0.001)) + math.log(0.001))
    return {
        'x_prompt': nrm(ks[0], (BATCH, SEQ, D_MODEL), f32),
        'x_sample': nrm(ks[1], (DEC_BATCH, DEC_SEQ, D_MODEL), f32),
        'cache_k': nrm(ks[2], (DEC_BATCH, DEPTH, PAST_LEN, H_A, 2 * DH_A), f32),
        'cache_v': nrm(ks[3], (DEC_BATCH, DEPTH, PAST_LEN, H_A, DV_A), f32),
        'state_delta': 0.5 * nrm(ks[4], (DEC_BATCH, DEPTH, 2, H_B, DK_B, DV_B), f32),
        'c': nrm(ks[5], (DEC_BATCH, D_MODEL), f32),
        'c_ctx': nrm(ks[6], (D_MODEL,), f32),
        'w_mod': nrm(ks[7], (DEPTH, D_MODEL, 6 * D_MODEL), f32) * D_MODEL ** -0.5,
        'b_mod': 0.01 * nrm(ks[8], (DEPTH, 6 * D_MODEL), f32),
        'norm_gains': 1.0 + 0.05 * nrm(ks[9], (DEPTH, 4, D_MODEL), f32),
        'w_in': nrm(ks[10], (DEPTH, D_MODEL, IN_W), f32) * D_MODEL ** -0.5,
        'lambda_qk': 0.1 * nrm(ks[11], (DEPTH, 4, DH_A), f32),
        'g_diff': 1.0 + 0.05 * nrm(ks[12], (DEPTH, DV_A), f32),
        'conv_qkv': nrm(ks[13], (DEPTH, SHORT_CONV, H_B * (2 * DK_B + DV_B)), f32) * SHORT_CONV ** -0.5,
        'a_log': jnp.log(jax.random.uniform(ks[14], (DEPTH, 2, H_B), dtype=f32, minval=1.0, maxval=16.0)),
        'dt_bias': dt + jnp.log(-jnp.expm1(-dt)),
        'g_delta': 1.0 + 0.05 * nrm(ks[16], (DEPTH, DV_B), f32),
        'w_pool': nrm(ks[17], (DEPTH, N_POOL, POOL_CH, POOL_CH), f32) * POOL_CH ** -0.5,
        'pool_scale': 1.0 + 0.05 * nrm(ks[18], (DEPTH, W_C), f32),
        'w_out': nrm(ks[19], (DEPTH, MIX_W, D_MODEL), f32) * MIX_W ** -0.5,
        'w_up': nrm(ks[20], (DEPTH, D_MODEL, 2 * D_FF), f32) * D_MODEL ** -0.5,
        'conv_ffn': nrm(ks[21], (DEPTH, FFN_CONV, 2 * D_FF), f32) * FFN_CONV ** -0.5,
        'w_down': nrm(ks[22], (DEPTH, D_FF, D_MODEL), f32) * D_FF ** -0.5,
    }


def reference(x_prompt, x_sample, cache_k, cache_v, state_delta, c, c_ctx, w_mod, b_mod, norm_gains, w_in,
              lambda_qk, g_diff, conv_qkv, a_log, dt_bias, g_delta, w_pool, pool_scale, w_out, w_up, conv_ffn, w_down):
    p = {'w_mod': w_mod, 'b_mod': b_mod, 'norm_gains': norm_gains, 'w_in': w_in, 'lambda_qk': lambda_qk,
         'g_diff': g_diff, 'conv_qkv': conv_qkv, 'a_log': a_log, 'dt_bias': dt_bias, 'g_delta': g_delta,
         'w_pool': w_pool, 'pool_scale': pool_scale, 'w_out': w_out, 'w_up': w_up, 'conv_ffn': conv_ffn,
         'w_down': w_down}
    xp = x_prompt
    ks, vs, ss = [], [], []
    for l in range(DEPTH):
        xp, (k_l, v_l, s_l) = layer(xp, c_ctx[None, None, :], p, l, None)
        ks.append(k_l)
        vs.append(v_l)
        ss.append(s_l)
    y_prompt = xp
    new_cache_k = jnp.stack(ks, axis=1)
    new_cache_v = jnp.stack(vs, axis=1)
    new_state_delta = jnp.stack(ss, axis=1)
    xs = x_sample
    for l in range(DEPTH):
        xs, _ = layer(xs, c[:, None, :], p, l, (cache_k[:, l], cache_v[:, l], state_delta[:, l]))
    y_sample = xs
    return (y_prompt, y_sample, new_cache_k, new_cache_v, new_state_delta)
```

```python
import functools
import math

import jax
import jax.numpy as jnp
from jax import lax
from jax.experimental import pallas as pl
from jax.experimental.pallas import tpu as pltpu

D_MODEL = 2048
DEPTH = 2
GRID_W = 64
H_A = 6
DH_A = 64
DV_A = 2 * DH_A
H_B = 6
DK_B = 128
DV_B = 128
CHUNK = 64
N_POOL = 4
POOL_CH = 128
POOL_WINDOWS = (2, 4, 8, 16)
W_A = H_A * DV_A
W_B = H_B * DV_B
W_C = N_POOL * POOL_CH
D_FF = 5632
ROPE_THETA = 10000.0
EPS = 1e-6

LANES = 128
SUBLANES = 8
SEG = 768
QKVB_W = H_B * (2 * DK_B + DV_B)
Z_W = 8 * SEG
POOL_COL = 6 * SEG
GATE_COL = POOL_COL + W_C
ZB_SEG = 7
N_MOD = 6 * D_MODEL
VMEM_PHYS = 64 * 1024 * 1024

F32 = jnp.float32
BF16 = jnp.bfloat16


def _cparams(sem, vmem_mb):
    assert vmem_mb * 1024 * 1024 < VMEM_PHYS
    return pltpu.CompilerParams(dimension_semantics=sem, vmem_limit_bytes=vmem_mb * 1024 * 1024)


def _silu(x):
    return x * (1.0 / (1.0 + jnp.exp(-x)))


def _dot(a, b):
    return jnp.dot(a, b, preferred_element_type=F32)


def _dot_nt(a, b):
    return lax.dot_general(a, b, (((1,), (1,)), ((), ())), preferred_element_type=F32)


def _dot_tn(a, b):
    return lax.dot_general(a, b, (((0,), (0,)), ((), ())), preferred_element_type=F32)


def _mod_kernel(c_ref, w_ref, b_ref, o_ref):
    a = _silu(c_ref[...]).astype(BF16)
    o_ref[...] = _dot(a, w_ref[...].astype(BF16)) + b_ref[...]


def _modulation(cond8, w_mod, b_mod):
    tn = 1024
    return pl.pallas_call(
        _mod_kernel,
        grid=(DEPTH, N_MOD // tn),
        in_specs=[
            pl.BlockSpec((SUBLANES, D_MODEL), lambda l, j: (0, 0)),
            pl.BlockSpec((None, D_MODEL, tn), lambda l, j: (l, 0, j)),
            pl.BlockSpec((None, 1, tn), lambda l, j: (l, 0, j)),
        ],
        out_specs=pl.BlockSpec((None, SUBLANES, tn), lambda l, j: (l, 0, j)),
        out_shape=jax.ShapeDtypeStruct((DEPTH, SUBLANES, N_MOD), F32),
        compiler_params=_cparams(("parallel", "parallel"), 40),
        name="modulation",
    )(cond8, w_mod, b_mod.reshape(DEPTH, 1, N_MOD))


def _modulated_norm(x, g, sc, sh):
    ms = jnp.mean(x * x, axis=-1, keepdims=True)
    return (x * lax.rsqrt(ms + EPS) * g) * (1.0 + sc) + sh


def _in_proj_kernel(x_ref, g_ref, sc_ref, sh_ref, w_ref, o_ref, h_ref, *, tm):
    @pl.when(pl.program_id(1) == 0)
    def _():
        rc = 256
        for r in range(tm // rc):
            x = x_ref[r * rc:(r + 1) * rc, :]
            h_ref[r * rc:(r + 1) * rc, :] = _modulated_norm(x, g_ref[...], sc_ref[...], sh_ref[...]).astype(BF16)

    o_ref[...] = _dot(h_ref[...], w_ref[...])


def _mod_spec(which, row_of_tile):
    return pl.BlockSpec((None, None, 1, D_MODEL), lambda i, *_: (row_of_tile(i), which, 0, 0))


def _gain_spec(k):
    return pl.BlockSpec((None, 1, D_MODEL), lambda i, *_: (k, 0, 0))


def _in_proj(x, gains, mod, w_in_p, row_of_tile, tm):
    t = x.shape[0]
    return pl.pallas_call(
        functools.partial(_in_proj_kernel, tm=tm),
        grid=(t // tm, Z_W // SEG),
        in_specs=[
            pl.BlockSpec((tm, D_MODEL), lambda i, j: (i, 0)),
            _gain_spec(0),
            _mod_spec(1, row_of_tile),
            _mod_spec(0, row_of_tile),
            pl.BlockSpec((D_MODEL, SEG), lambda i, j: (0, j)),
        ],
        out_specs=pl.BlockSpec((tm, SEG), lambda i, j: (i, j)),
        out_shape=jax.ShapeDtypeStruct((t, Z_W), F32),
        scratch_shapes=[pltpu.VMEM((tm, D_MODEL), BF16)],
        compiler_params=_cparams(("parallel", "arbitrary"), 48),
        name="in_proj",
    )(x, gains, mod, mod, w_in_p)


def _rope_tables(n):
    rows = n // GRID_W
    row = jnp.repeat(jnp.arange(rows), GRID_W)
    col = jnp.arange(rows * GRID_W) % GRID_W
    n_pair = DH_A // 4
    inv = ROPE_THETA ** (-jnp.arange(n_pair, dtype=F32) / n_pair)
    ang = jnp.concatenate([row[:, None] * inv, col[:, None] * inv], axis=-1)
    cos = jnp.repeat(jnp.cos(ang), 2, axis=-1)
    sin = jnp.repeat(jnp.sin(ang), 2, axis=-1) * jnp.tile(jnp.array([-1.0, 1.0], F32), DH_A // 2)
    return jnp.tile(cos, (1, 2)), jnp.tile(sin, (1, 2))


def _rope_kernel(q_ref, k_ref, v_ref, cos_ref, sin_ref, qo_ref, ko_ref, vo_ref):
    cos = cos_ref[...]
    sin = sin_ref[...]
    even = (lax.broadcasted_iota(jnp.int32, cos.shape, 1) % 2) == 0
    for h in range(H_A):
        sl = slice(h * LANES, (h + 1) * LANES)
        for src, dst, scale in ((q_ref, qo_ref, DH_A ** -0.5), (k_ref, ko_ref, None)):
            x = src[:, sl]
            swapped = jnp.where(even, pltpu.roll(x, LANES - 1, 1), pltpu.roll(x, 1, 1))
            y = x * cos + swapped * sin
            if scale is not None:
                y = y * scale
            dst[:, sl] = y.astype(BF16)
    vo_ref[...] = v_ref[...].astype(BF16)


def _rope_prep(z, seqlen):
    t = z.shape[0]
    tm = 512
    cos, sin = _rope_tables(seqlen)
    per_seq = seqlen // tm
    tab_spec = pl.BlockSpec((tm, LANES), lambda i: (i % per_seq, 0))
    out = jax.ShapeDtypeStruct((t, SEG), BF16)
    return pl.pallas_call(
        _rope_kernel,
        grid=(t // tm,),
        in_specs=[pl.BlockSpec((tm, SEG), lambda i: (i, 0)),
                  pl.BlockSpec((tm, SEG), lambda i: (i, 1)),
                  pl.BlockSpec((tm, SEG), lambda i: (i, 2)),
                  tab_spec, tab_spec],
        out_specs=[pl.BlockSpec((tm, SEG), lambda i: (i, 0))] * 3,
        out_shape=[out, out, out],
        compiler_params=_cparams(("parallel",), 32),
        name="rope_prep",
    )(z, z, z, cos, sin)


def _attn_kernel(*refs, n_pieces, q_prepped, lam_init):
    lq_ref, gd_ref, q_ref = refs[:3]
    kv_refs = refs[3:3 + 2 * n_pieces]
    o_ref = refs[3 + 2 * n_pieces]
    lq = lq_ref[...]
    lam = (jnp.exp(jnp.sum(lq[0:1] * lq[1:2], axis=-1, keepdims=True))
           - jnp.exp(jnp.sum(lq[2:3] * lq[3:4], axis=-1, keepdims=True)) + lam_init)
    q = q_ref[...]
    tq = q.shape[0]
    if not q_prepped:
        q = q * (DH_A ** -0.5)
    lo = lax.broadcasted_iota(jnp.int32, q.shape, 1) < DH_A
    zero = jnp.zeros_like(q)
    qq = jnp.concatenate([jnp.where(lo, q, zero), jnp.where(lo, zero, q)], axis=0).astype(BF16)
    scores = [_dot_nt(qq, kv_refs[2 * p][...].astype(BF16)) for p in range(n_pieces)]
    m = scores[0].max(axis=-1, keepdims=True)
    for s in scores[1:]:
        m = jnp.maximum(m, s.max(axis=-1, keepdims=True))
    es = [jnp.exp(s - m) for s in scores]
    l = es[0].sum(axis=-1, keepdims=True)
    for e in es[1:]:
        l = l + e.sum(axis=-1, keepdims=True)
    r = 1.0 / l
    r1 = r[:tq]
    r2 = r[tq:] * lam
    o = None
    for p in range(n_pieces):
        pr = (es[p][:tq] * r1 - es[p][tq:] * r2).astype(BF16)
        t = _dot(pr, kv_refs[2 * p + 1][...].astype(BF16))
        o = t if o is None else o + t
    ms = jnp.mean(o * o, axis=-1, keepdims=True)
    o_ref[...] = o * lax.rsqrt(ms + EPS) * gd_ref[...] * (1.0 - lam_init)


def _attention(q_arr, q_col0, pieces, lambda_qk_l, g_diff_l, n_seq, seqlen, tq, q_prepped, lam_init):
    per_seq = seqlen // tq
    in_specs = [
        pl.BlockSpec((4, DH_A), lambda b, h, i: (0, 0)),
        pl.BlockSpec((1, DV_A), lambda b, h, i: (0, 0)),
        pl.BlockSpec((tq, LANES), lambda b, h, i: (b * per_seq + i, q_col0 + h)),
    ]
    args = [lambda_qk_l, g_diff_l.reshape(1, DV_A), q_arr]
    for k_arr, v_arr, k_spec, v_spec in pieces:
        in_specs += [k_spec, v_spec]
        args += [k_arr, v_arr]
    return pl.pallas_call(
        functools.partial(_attn_kernel, n_pieces=len(pieces), q_prepped=q_prepped, lam_init=lam_init),
        grid=(n_seq, H_A, per_seq),
        in_specs=in_specs,
        out_specs=pl.BlockSpec((tq, LANES), lambda b, h, i: (b * per_seq + i, h)),
        out_shape=jax.ShapeDtypeStruct((n_seq * seqlen, W_A), F32),
        compiler_params=_cparams(("parallel", "parallel", "arbitrary"), 48),
        name="diff_attention",
    )(*args)


def _pool_kernel(u_ref, w_ref, sc_ref, o_ref, *, seqlen):
    rows = u_ref.shape[0]
    pos = lax.broadcasted_iota(jnp.int32, (rows, POOL_CH), 0) % seqlen
    for g, win in enumerate(POOL_WINDOWS):
        sl = slice(g * POOL_CH, (g + 1) * POOL_CH)
        u = u_ref[:, sl]
        half = win // 2
        acc = u
        for d in range(-half, half):
            if d == 0:
                continue
            shifted = pltpu.roll(u, (-d) % rows, 0)
            valid = (pos + d >= 0) & (pos + d < seqlen)
            acc = acc + jnp.where(valid, shifted, 0.0)
        cnt = (jnp.minimum(pos + half, seqlen) - jnp.maximum(pos - half, 0)).astype(F32)
        y = acc / cnt - u
        o_ref[:, sl] = _dot(y.astype(BF16), w_ref[g].astype(BF16)) * sc_ref[:, sl]


def _pool(z, w_pool_l, pool_scale_l, seqlen, rows):
    t = z.shape[0]
    col_block = POOL_COL // W_C
    assert col_block * W_C == POOL_COL
    return pl.pallas_call(
        functools.partial(_pool_kernel, seqlen=seqlen),
        grid=(t // rows,),
        in_specs=[pl.BlockSpec((rows, W_C), lambda i: (i, col_block)),
                  pl.BlockSpec((N_POOL, POOL_CH, POOL_CH), lambda i: (0, 0, 0)),
                  pl.BlockSpec((1, W_C), lambda i: (0, 0))],
        out_specs=pl.BlockSpec((rows, W_C), lambda i: (i, 0)),
        out_shape=jax.ShapeDtypeStruct((t, W_C), F32),
        compiler_params=_cparams(("parallel",), 48),
        name="pool_mixer",
    )(z, w_pool_l, pool_scale_l.reshape(1, W_C))


def _shift_rows(x, prev_row, next_row, pos, seqlen):
    tm = x.shape[0]
    ridx = lax.broadcasted_iota(jnp.int32, x.shape, 0)
    xm = jnp.where(ridx == 0, prev_row, pltpu.roll(x, 1, 0))
    xp = jnp.where(ridx == tm - 1, next_row, pltpu.roll(x, tm - 1, 0))
    xm = jnp.where(pos == 0, 0.0, xm)
    xp = jnp.where(pos == seqlen - 1, 0.0, xp)
    return xm, xp


def _halo_specs(tm, width, col_block, t):
    nb = t // SUBLANES
    per = tm // SUBLANES
    prev = pl.BlockSpec((SUBLANES, width), lambda i, *_: (jnp.maximum(i * per - 1, 0), col_block))
    nxt = pl.BlockSpec((SUBLANES, width), lambda i, *_: (jnp.minimum((i + 1) * per, nb - 1), col_block))
    return prev, nxt


def _dn_prep_kernel(x_ref, xprev_ref, xnext_ref, gate_ref, cw_ref, gp_ref, qkv_ref, gb_ref, *, seqlen, tm):
    i = pl.program_id(0)
    cw = cw_ref[...]
    cblk = 256
    for cb in range(QKVB_W // cblk):
        sl = slice(cb * cblk, (cb + 1) * cblk)
        x = x_ref[:, sl]
        pos = (i * tm + lax.broadcasted_iota(jnp.int32, x.shape, 0)) % seqlen
        xm, xp = _shift_rows(x, xprev_ref[SUBLANES - 1:SUBLANES, sl], xnext_ref[0:1, sl], pos, seqlen)
        y = _silu(xm * cw[0:1, sl] + x * cw[1:2, sl] + xp * cw[2:3, sl])
        for hh in range(cblk // LANES):
            c0 = cb * cblk + hh * LANES
            yh = y[:, hh * LANES:(hh + 1) * LANES]
            if c0 < 2 * H_B * DK_B:
                yh = yh * lax.rsqrt(jnp.sum(yh * yh, axis=-1, keepdims=True) + EPS)
                if c0 < H_B * DK_B:
                    yh = yh * (DK_B ** -0.5)
            qkv_ref[:, c0:c0 + LANES] = yh
    lane = lax.broadcasted_iota(jnp.int32, (tm, LANES), 1)
    for d in range(2):
        raw = gate_ref[:, d * LANES:(d + 1) * LANES]
        gp = gp_ref[d]
        beta = 1.0 / (1.0 + jnp.exp(-raw))
        xa = raw + gp[1:2]
        softplus = jnp.maximum(xa, 0.0) + jnp.log1p(jnp.exp(-jnp.abs(xa)))
        gdec = -jnp.exp(gp[0:1]) * softplus
        gb_ref[d] = jnp.where(lane < H_B, beta, jnp.where(lane < 2 * H_B, gdec, 0.0))


def _dn_prep(z, conv_qkv_l, gate_params, seqlen, tm):
    t = z.shape[0]
    prev, nxt = _halo_specs(tm, QKVB_W, 1, t)
    return pl.pallas_call(
        functools.partial(_dn_prep_kernel, seqlen=seqlen, tm=tm),
        grid=(t // tm,),
        in_specs=[pl.BlockSpec((tm, QKVB_W), lambda i: (i, 1)), prev, nxt,
                  pl.BlockSpec((tm, 2 * LANES), lambda i: (i, GATE_COL // (2 * LANES))),
                  pl.BlockSpec((3, QKVB_W), lambda i: (0, 0)),
                  pl.BlockSpec((2, 2, LANES), lambda i: (0, 0, 0))],
        out_specs=[pl.BlockSpec((tm, QKVB_W), lambda i: (i, 0)),
                   pl.BlockSpec((2, tm, LANES), lambda i: (0, i, 0))],
        out_shape=[jax.ShapeDtypeStruct((t, QKVB_W), F32), jax.ShapeDtypeStruct((2, t, LANES), F32)],
        compiler_params=_cparams(("parallel",), 48),
        name="deltanet_prep",
    )(z, z, z, z, conv_qkv_l, gate_params)


def _dn_scan_kernel(*refs, cs, has_s0):
    if has_s0:
        qkv_ref, gb_ref, s0_ref, o_ref, sfin_ref, s_scr = refs
    else:
        qkv_ref, gb_ref, o_ref, sfin_ref, s_scr = refs
    d = pl.program_id(1)
    st = pl.program_id(2)

    @pl.when(st == 0)
    def _():
        if has_s0:
            s_scr[...] = s0_ref[...]
        else:
            s_scr[...] = jnp.zeros_like(s_scr)

    sgn = jnp.where(d == 0, 1, -1)
    ii = lax.broadcasted_iota(jnp.int32, (CHUNK, CHUNK), 0)
    jj = lax.broadcasted_iota(jnp.int32, (CHUNK, CHUNK), 1)
    rel = (ii - jj) * sgn
    incl = rel >= 0
    strict = rel > 0
    tri = incl.astype(F32)
    eye = (ii == jj).astype(F32)
    eq_masks = [((ii >> (3 + k)) == (jj >> (3 + k))).astype(F32) for k in range(3)] + [jnp.ones((CHUNK, CHUNK), F32)]
    for ci in range(cs):
        c = jnp.where(d == 0, ci, cs - 1 - ci)
        off = pl.multiple_of(c * CHUNK, CHUNK)
        gbc = gb_ref[pl.ds(off, CHUNK), :]
        gcol = jnp.dot(tri, gbc, preferred_element_type=F32, precision=lax.Precision.HIGHEST)
        grow = gcol.T
        gtot = jnp.sum(gbc, axis=0, keepdims=True)
        for h in range(H_B):
            q = qkv_ref[pl.ds(off, CHUNK), h * DK_B:(h + 1) * DK_B]
            k = qkv_ref[pl.ds(off, CHUNK), (H_B + h) * DK_B:(H_B + h + 1) * DK_B]
            v = qkv_ref[pl.ds(off, CHUNK), (2 * H_B + h) * DK_B:(2 * H_B + h + 1) * DK_B]
            beta = gbc[:, h:h + 1]
            gc = gcol[:, H_B + h:H_B + h + 1]
            gr = grow[H_B + h:H_B + h + 1, :]
            gt = gtot[:, H_B + h:H_B + h + 1]
            eg = jnp.exp(gc)
            kb16 = k.astype(BF16)
            kq = _dot_nt(jnp.concatenate([k, q], axis=0).astype(BF16), kb16)
            dec = jnp.where(incl, jnp.exp(jnp.where(incl, gc - gr, 0.0)), 0.0)
            lmat = jnp.where(strict, kq[:CHUNK] * dec, 0.0) * beta
            a_intra = kq[CHUNK:] * dec
            x = jnp.concatenate([v * beta, k * (beta * eg)], axis=1)
            lb = (lmat * eq_masks[0]).astype(BF16)
            p2f = _dot(lb, lb)
            p2 = p2f.astype(BF16)
            tinv = eye - lmat * eq_masks[0] + p2f - _dot(lb, p2)
            tinv = tinv + _dot(tinv.astype(BF16), _dot(p2, p2).astype(BF16))
            for lvl in range(3):
                c_off = (lmat * (eq_masks[lvl + 1] - eq_masks[lvl])).astype(BF16)
                t16 = tinv.astype(BF16)
                tinv = tinv - _dot(t16, _dot(c_off, t16).astype(BF16))
            x = x + _dot((tinv - eye).astype(BF16), x.astype(BF16))
            u = x[:, :DV_B]
            w = x[:, DV_B:]
            s = s_scr[h]
            r = _dot(jnp.concatenate([w, q * eg], axis=0).astype(BF16), s.astype(BF16))
            v_new = u - r[:CHUNK]
            vn16 = v_new.astype(BF16)
            o = r[CHUNK:] + _dot(a_intra.astype(BF16), vn16)
            k_dec = k * jnp.exp(gt - gc)
            s_scr[h] = s * jnp.exp(gt) + _dot_tn(k_dec.astype(BF16), vn16)
            o_ref[pl.ds(off, CHUNK), h * DV_B:(h + 1) * DV_B] = o

    @pl.when(st == pl.num_programs(2) - 1)
    def _():
        sfin_ref[...] = s_scr[...]


def _dn_scan(qkv, gb, s0, layer, n_seq, seqlen, cs):
    t = qkv.shape[0]
    rows = cs * CHUNK
    nsteps = seqlen // rows

    def rb(b, d, s):
        return b * nsteps + jnp.where(d == 0, s, nsteps - 1 - s)

    in_specs = [pl.BlockSpec((rows, QKVB_W), lambda b, d, s: (rb(b, d, s), 0)),
                pl.BlockSpec((None, rows, LANES), lambda b, d, s: (d, rb(b, d, s), 0))]
    args = [qkv, gb]
    if s0 is not None:
        in_specs.append(pl.BlockSpec((None, None, None, H_B, DK_B, DV_B), lambda b, d, s: (b, layer, d, 0, 0, 0)))
        args.append(s0)
    return pl.pallas_call(
        functools.partial(_dn_scan_kernel, cs=cs, has_s0=s0 is not None),
        grid=(n_seq, 2, nsteps),
        in_specs=in_specs,
        out_specs=[pl.BlockSpec((None, rows, W_B), lambda b, d, s: (d, rb(b, d, s), 0)),
                   pl.BlockSpec((None, None, H_B, DK_B, DV_B), lambda b, d, s: (b, d, 0, 0, 0))],
        out_shape=[jax.ShapeDtypeStruct((2, t, W_B), F32),
                   jax.ShapeDtypeStruct((n_seq, 2, H_B, DK_B, DV_B), F32)],
        scratch_shapes=[pltpu.VMEM((H_B, DK_B, DV_B), F32)],
        compiler_params=_cparams(("parallel", "parallel", "arbitrary"), 32),
        name="deltanet_scan",
    )(*args)


def _out_proj_kernel(oa_ref, od_ref, z_ref, oc_ref, x_ref, gt_ref, g1_ref, gd_ref, w_ref, o_ref, mix_ref):
    mix_ref[:, 0:W_A] = oa_ref[...].astype(BF16)
    gd = gd_ref[...]
    for h in range(H_B):
        sl = slice(h * DV_B, (h + 1) * DV_B)
        ob = od_ref[0, :, sl] + od_ref[1, :, sl]
        ms = jnp.mean(ob * ob, axis=-1, keepdims=True)
        ob = ob * lax.rsqrt(ms + EPS) * gd * _silu(z_ref[:, sl])
        mix_ref[:, W_A + h * DV_B:W_A + (h + 1) * DV_B] = ob.astype(BF16)
    mix_ref[:, W_A + W_B:] = oc_ref[...].astype(BF16)
    y = _dot(mix_ref[...], w_ref[...])
    ms = jnp.mean(y * y, axis=-1, keepdims=True)
    o_ref[...] = x_ref[...] + gt_ref[...] * (y * lax.rsqrt(ms + EPS) * g1_ref[...])


def _out_proj(oa, od, z, oc, x, gains, mod, g_delta_l, w_out_p, row_of_tile, tm):
    t = x.shape[0]
    return pl.pallas_call(
        _out_proj_kernel,
        grid=(t // tm,),
        in_specs=[pl.BlockSpec((tm, W_A), lambda i: (i, 0)),
                  pl.BlockSpec((2, tm, W_B), lambda i: (0, i, 0)),
                  pl.BlockSpec((tm, SEG), lambda i: (i, ZB_SEG)),
                  pl.BlockSpec((tm, W_C), lambda i: (i, 0)),
                  pl.BlockSpec((tm, D_MODEL), lambda i: (i, 0)),
                  _mod_spec(2, row_of_tile),
                  _gain_spec(1),
                  pl.BlockSpec((1, DV_B), lambda i: (0, 0)),
                  pl.BlockSpec((D_MODEL, D_MODEL), lambda i: (0, 0))],
        out_specs=pl.BlockSpec((tm, D_MODEL), lambda i: (i, 0)),
        out_shape=jax.ShapeDtypeStruct((t, D_MODEL), F32),
        scratch_shapes=[pltpu.VMEM((tm, D_MODEL), BF16)],
        compiler_params=_cparams(("parallel",), 56),
        name="out_proj",
    )(oa, od, z, oc, x, mod, gains, g_delta_l.reshape(1, DV_B), w_out_p)


def _ffn_kernel(x_ref, xprev_ref, xnext_ref, g2_ref, sc_ref, sh_ref, gt_ref, g3_ref,
                wg_ref, wu_ref, cg_ref, cu_ref, wd_ref, o_ref, h_ref, p_ref, acc_ref, *, seqlen, tm):
    i = pl.program_id(0)
    j = pl.program_id(1)
    hb = SUBLANES

    @pl.when(j == 0)
    def _():
        norm = lambda v: _modulated_norm(v, g2_ref[...], sc_ref[...], sh_ref[...]).astype(BF16)
        h_ref[0:hb, :] = norm(xprev_ref[...])
        rc = 256
        for r in range(tm // rc):
            h_ref[hb + r * rc:hb + (r + 1) * rc, :] = norm(x_ref[r * rc:(r + 1) * rc, :])
        h_ref[hb + tm:, :] = norm(xnext_ref[...])
        acc_ref[...] = jnp.zeros_like(acc_ref)

    tf = wg_ref.shape[1]
    pos = (i * tm + lax.broadcasted_iota(jnp.int32, (tm, tf), 0)) % seqlen
    has_prev = pos != 0
    has_next = pos != seqlen - 1

    def conv_branch(w_ref, c_ref):
        p_ref[...] = _dot(h_ref[...], w_ref[...])
        cw = c_ref[...]
        return (jnp.where(has_prev, p_ref[hb - 1:hb - 1 + tm, :], 0.0) * cw[0:1]
                + p_ref[hb:hb + tm, :] * cw[1:2]
                + jnp.where(has_next, p_ref[hb + 1:hb + 1 + tm, :], 0.0) * cw[2:3])

    gate = conv_branch(wg_ref, cg_ref)
    up = conv_branch(wu_ref, cu_ref)
    acc_ref[...] += _dot((_silu(gate) * up).astype(BF16), wd_ref[...])

    @pl.when(j == pl.num_programs(1) - 1)
    def _():
        f = acc_ref[...]
        ms = jnp.mean(f * f, axis=-1, keepdims=True)
        o_ref[...] = x_ref[...] + gt_ref[...] * (f * lax.rsqrt(ms + EPS) * g3_ref[...])


def _ffn(x, gains, mod, w_up_p, conv_ffn_l, w_down_p, row_of_tile, seqlen, tm, tf):
    t = x.shape[0]
    nf = D_FF // tf
    prev, nxt = _halo_specs(tm, D_MODEL, 0, t)
    return pl.pallas_call(
        functools.partial(_ffn_kernel, seqlen=seqlen, tm=tm),
        grid=(t // tm, nf),
        in_specs=[pl.BlockSpec((tm, D_MODEL), lambda i, j: (i, 0)), prev, nxt,
                  _gain_spec(2), _mod_spec(4, row_of_tile), _mod_spec(3, row_of_tile), _mod_spec(5, row_of_tile),
                  _gain_spec(3),
                  pl.BlockSpec((D_MODEL, tf), lambda i, j: (0, j)),
                  pl.BlockSpec((D_MODEL, tf), lambda i, j: (0, nf + j)),
                  pl.BlockSpec((3, tf), lambda i, j: (0, j)),
                  pl.BlockSpec((3, tf), lambda i, j: (0, nf + j)),
                  pl.BlockSpec((tf, D_MODEL), lambda i, j: (j, 0))],
        out_specs=pl.BlockSpec((tm, D_MODEL), lambda i, j: (i, 0)),
        out_shape=jax.ShapeDtypeStruct((t, D_MODEL), F32),
        scratch_shapes=[pltpu.VMEM((tm + 2 * SUBLANES, D_MODEL), BF16),
                        pltpu.VMEM((tm + 2 * SUBLANES, tf), F32),
                        pltpu.VMEM((tm, D_MODEL), F32)],
        compiler_params=_cparams(("parallel", "arbitrary"), 56),
        name="conv_ffn",
    )(x, x, x, gains, mod, mod, mod, gains, w_up_p, w_up_p, conv_ffn_l, conv_ffn_l, w_down_p)


def _pack_w_in(w_in_l):
    main = w_in_l[:, :6 * SEG]
    zb = w_in_l[:, 6 * SEG:7 * SEG]
    beta = w_in_l[:, 7 * SEG:7 * SEG + 2 * H_B]
    alpha = w_in_l[:, 7 * SEG + 2 * H_B:7 * SEG + 4 * H_B]
    pool = w_in_l[:, 7 * SEG + 4 * H_B:]
    pad = jnp.zeros((D_MODEL, LANES - 2 * H_B), w_in_l.dtype)
    cols = [main, pool]
    for d in range(2):
        cols += [beta[:, d * H_B:(d + 1) * H_B], alpha[:, d * H_B:(d + 1) * H_B], pad]
    cols.append(zb)
    return jnp.concatenate(cols, axis=1).astype(BF16)


def _gate_params(a_log_l, dt_bias_l):
    gp = jnp.zeros((2, 2, LANES), F32)
    gp = gp.at[:, 0, H_B:2 * H_B].set(a_log_l.astype(F32))
    return gp.at[:, 1, H_B:2 * H_B].set(dt_bias_l.astype(F32))


def _layer(x, mod_l, row_of_tile_fn, p, l, n_seq, seqlen, ctx):
    t = x.shape[0]
    gains = p['norm_gains'][l].reshape(4, 1, D_MODEL)
    lam_init = 0.8 - 0.6 * math.exp(-0.3 * l)
    tm_big = 1024
    assert t % tm_big == 0 and (seqlen % tm_big == 0 or tm_big % seqlen == 0)
    z = _in_proj(x, gains, mod_l, p['w_in_p'][l], row_of_tile_fn(tm_big), tm_big)

    if ctx is None:
        kv_spec = lambda col: pl.BlockSpec((seqlen, LANES), lambda b, h, i: (b, col * H_A + h))
        pieces = [(z, z, kv_spec(1), kv_spec(2))]
        oa = _attention(z, 0, pieces, p['lambda_qk'][l], p['g_diff'][l], n_seq, seqlen, min(256, seqlen), False,
                        lam_init)
    else:
        cache_k, cache_v, state_delta = ctx
        past = cache_k.shape[2]
        q_r, k_r, v_r = _rope_prep(z, seqlen)
        lat_spec = pl.BlockSpec((seqlen, LANES), lambda b, h, i: (b, h))
        ck = cache_k.reshape(n_seq, DEPTH, past, H_A * 2 * DH_A)
        cv = cache_v.reshape(n_seq, DEPTH, past, H_A * DV_A)
        c_spec = pl.BlockSpec((None, None, past, LANES), lambda b, h, i: (b, l, 0, h))
        pieces = [(k_r, v_r, lat_spec, lat_spec), (ck, cv, c_spec, c_spec)]
        oa = _attention(q_r, 0, pieces, p['lambda_qk'][l], p['g_diff'][l], n_seq, seqlen, 128, True, lam_init)

    qkv, gb = _dn_prep(z, p['conv_qkv'][l], _gate_params(p['a_log'][l], p['dt_bias'][l]), seqlen, tm_big)
    od, s_fin = _dn_scan(qkv, gb, None if ctx is None else ctx[2], l, n_seq, seqlen, 4)

    oc = _pool(z, p['w_pool'][l], p['pool_scale'][l], seqlen, max(seqlen, 1024))

    tm = 256
    x = _out_proj(oa, od, z, oc, x, gains, mod_l, p['g_delta'][l], p['w_out_p'][l], row_of_tile_fn(tm), tm)
    tm = 512
    x = _ffn(x, gains, mod_l, p['w_up_p'][l], p['conv_ffn'][l], p['w_down_p'][l], row_of_tile_fn(tm), seqlen, tm, 512)
    if ctx is None:
        k_new = z[:, SEG:2 * SEG].reshape(n_seq, seqlen, H_A, 2 * DH_A)
        v_new = z[:, 2 * SEG:3 * SEG].reshape(n_seq, seqlen, H_A, DV_A)
        return x, (k_new, v_new, s_fin)
    return x, None


def kernel(x_prompt, x_sample, cache_k, cache_v, state_delta, c, c_ctx, w_mod, b_mod, norm_gains, w_in, lambda_qk,
           g_diff, conv_qkv, a_log, dt_bias, g_delta, w_pool, pool_scale, w_out, w_up, conv_ffn, w_down):
    batch, seq, _ = x_prompt.shape
    dec_batch, dec_seq, _ = x_sample.shape
    p = {'norm_gains': norm_gains, 'lambda_qk': lambda_qk, 'g_diff': g_diff, 'conv_qkv': conv_qkv, 'a_log': a_log,
         'dt_bias': dt_bias, 'g_delta': g_delta, 'w_pool': w_pool, 'pool_scale': pool_scale, 'conv_ffn': conv_ffn,
         'w_in_p': [_pack_w_in(w_in[l]) for l in range(DEPTH)],
         'w_out_p': w_out.astype(BF16), 'w_up_p': w_up.astype(BF16), 'w_down_p': w_down.astype(BF16)}

    assert 1 + dec_batch <= SUBLANES
    cond8 = jnp.concatenate([c_ctx[None, :], c, jnp.zeros((SUBLANES - 1 - dec_batch, D_MODEL), F32)], axis=0)
    mod = _modulation(cond8, w_mod, b_mod).reshape(DEPTH, SUBLANES, 6, 1, D_MODEL)

    ctx_rows = lambda tm: (lambda i: 0)
    lat_rows = lambda tm: (lambda i: 1 + (i * tm) // dec_seq)

    xp = x_prompt.reshape(batch * seq, D_MODEL)
    ks, vs, ss = [], [], []
    for l in range(DEPTH):
        xp, (k_l, v_l, s_l) = _layer(xp, mod[l], ctx_rows, p, l, batch, seq, None)
        ks.append(k_l)
        vs.append(v_l)
        ss.append(s_l)
    xs = x_sample.reshape(dec_batch * dec_seq, D_MODEL)
    for l in range(DEPTH):
        xs, _ = _layer(xs, mod[l], lat_rows, p, l, dec_batch, dec_seq, (cache_k, cache_v, state_delta))
    return (xp.reshape(batch, seq, D_MODEL), xs.reshape(dec_batch, dec_seq, D_MODEL),
            jnp.stack(ks, axis=1), jnp.stack(vs, axis=1), jnp.stack(ss, axis=1))
```

```python
import functools
import math

import jax
import jax.numpy as jnp
from jax import lax
from jax.experimental import pallas as pl
from jax.experimental.pallas import tpu as pltpu

D_MODEL = 2048
DEPTH = 2
GRID_W = 64
H_A = 6
DH_A = 64
DV_A = 2 * DH_A
H_B = 6
DK_B = 128
DV_B = 128
CHUNK = 64
N_POOL = 4
POOL_CH = 128
POOL_WINDOWS = (2, 4, 8, 16)
W_A = H_A * DV_A
W_B = H_B * DV_B
W_C = N_POOL * POOL_CH
D_FF = 5632
ROPE_THETA = 10000.0
EPS = 1e-6

LANES = 128
SUBLANES = 8
SEG = 768
QKVB_W = H_B * (2 * DK_B + DV_B)
Z_W = 8 * SEG
POOL_COL = 6 * SEG
GATE_COL = POOL_COL + W_C
ZB_SEG = 7
N_MOD = 6 * D_MODEL
VMEM_PHYS = 64 * 1024 * 1024

F32 = jnp.float32
BF16 = jnp.bfloat16


def _cparams(sem, vmem_mb):
    assert vmem_mb * 1024 * 1024 < VMEM_PHYS
    return pltpu.CompilerParams(dimension_semantics=sem, vmem_limit_bytes=vmem_mb * 1024 * 1024)


def _silu(x):
    return x * (1.0 / (1.0 + jnp.exp(-x)))


def _dot(a, b):
    return jnp.dot(a, b, preferred_element_type=F32)


def _dot_nt(a, b):
    return lax.dot_general(a, b, (((1,), (1,)), ((), ())), preferred_element_type=F32)


def _dot_tn(a, b):
    return lax.dot_general(a, b, (((0,), (0,)), ((), ())), preferred_element_type=F32)


def _mod_kernel(c_ref, w_ref, b_ref, o_ref):
    a = _silu(c_ref[...]).astype(BF16)
    o_ref[...] = _dot(a, w_ref[...].astype(BF16)) + b_ref[...]


def _modulation(cond8, w_mod, b_mod):
    tn = 1024
    return pl.pallas_call(
        _mod_kernel,
        grid=(DEPTH, N_MOD // tn),
        in_specs=[
            pl.BlockSpec((SUBLANES, D_MODEL), lambda l, j: (0, 0)),
            pl.BlockSpec((None, D_MODEL, tn), lambda l, j: (l, 0, j)),
            pl.BlockSpec((None, 1, tn), lambda l, j: (l, 0, j)),
        ],
        out_specs=pl.BlockSpec((None, SUBLANES, tn), lambda l, j: (l, 0, j)),
        out_shape=jax.ShapeDtypeStruct((DEPTH, SUBLANES, N_MOD), F32),
        compiler_params=_cparams(("parallel", "parallel"), 40),
        name="modulation",
    )(cond8, w_mod, b_mod.reshape(DEPTH, 1, N_MOD))


def _modulated_norm(x, g, sc, sh):
    ms = jnp.mean(x * x, axis=-1, keepdims=True)
    return (x * lax.rsqrt(ms + EPS) * g) * (1.0 + sc) + sh


def _in_proj_kernel(x_ref, g_ref, sc_ref, sh_ref, w_ref, o_ref, h_ref, *, tm):
    @pl.when(pl.program_id(1) == 0)
    def _():
        rc = 256
        for r in range(tm // rc):
            x = x_ref[r * rc:(r + 1) * rc, :]
            h_ref[r * rc:(r + 1) * rc, :] = _modulated_norm(x, g_ref[...], sc_ref[...], sh_ref[...]).astype(BF16)

    o_ref[...] = _dot(h_ref[...], w_ref[...])


def _mod_spec(which, row_of_tile):
    return pl.BlockSpec((None, None, 1, D_MODEL), lambda i, *_: (row_of_tile(i), which, 0, 0))


def _gain_spec(k):
    return pl.BlockSpec((None, 1, D_MODEL), lambda i, *_: (k, 0, 0))


def _in_proj(x, gains, mod, w_in_p, row_of_tile, tm):
    t = x.shape[0]
    return pl.pallas_call(
        functools.partial(_in_proj_kernel, tm=tm),
        grid=(t // tm, Z_W // SEG),
        in_specs=[
            pl.BlockSpec((tm, D_MODEL), lambda i, j: (i, 0)),
            _gain_spec(0),
            _mod_spec(1, row_of_tile),
            _mod_spec(0, row_of_tile),
            pl.BlockSpec((D_MODEL, SEG), lambda i, j: (0, j)),
        ],
        out_specs=pl.BlockSpec((tm, SEG), lambda i, j: (i, j)),
        out_shape=jax.ShapeDtypeStruct((t, Z_W), F32),
        scratch_shapes=[pltpu.VMEM((tm, D_MODEL), BF16)],
        compiler_params=_cparams(("parallel", "arbitrary"), 48),
        name="in_proj",
    )(x, gains, mod, mod, w_in_p)


def _rope_tables(n):
    rows = n // GRID_W
    row = jnp.repeat(jnp.arange(rows), GRID_W)
    col = jnp.arange(rows * GRID_W) % GRID_W
    n_pair = DH_A // 4
    inv = ROPE_THETA ** (-jnp.arange(n_pair, dtype=F32) / n_pair)
    ang = jnp.concatenate([row[:, None] * inv, col[:, None] * inv], axis=-1)
    cos = jnp.repeat(jnp.cos(ang), 2, axis=-1)
    sin = jnp.repeat(jnp.sin(ang), 2, axis=-1) * jnp.tile(jnp.array([-1.0, 1.0], F32), DH_A // 2)
    return jnp.tile(cos, (1, 2)), jnp.tile(sin, (1, 2))


def _rope_kernel(q_ref, k_ref, v_ref, cos_ref, sin_ref, qo_ref, ko_ref, vo_ref):
    cos = cos_ref[...]
    sin = sin_ref[...]
    even = (lax.broadcasted_iota(jnp.int32, cos.shape, 1) % 2) == 0
    for h in range(H_A):
        sl = slice(h * LANES, (h + 1) * LANES)
        for src, dst, scale in ((q_ref, qo_ref, DH_A ** -0.5), (k_ref, ko_ref, None)):
            x = src[:, sl]
            swapped = jnp.where(even, pltpu.roll(x, LANES - 1, 1), pltpu.roll(x, 1, 1))
            y = x * cos + swapped * sin
            if scale is not None:
                y = y * scale
            dst[:, sl] = y.astype(BF16)
    vo_ref[...] = v_ref[...].astype(BF16)


def _rope_prep(z, seqlen):
    t = z.shape[0]
    tm = 512
    cos, sin = _rope_tables(seqlen)
    per_seq = seqlen // tm
    tab_spec = pl.BlockSpec((tm, LANES), lambda i: (i % per_seq, 0))
    out = jax.ShapeDtypeStruct((t, SEG), BF16)
    return pl.pallas_call(
        _rope_kernel,
        grid=(t // tm,),
        in_specs=[pl.BlockSpec((tm, SEG), lambda i: (i, 0)),
                  pl.BlockSpec((tm, SEG), lambda i: (i, 1)),
                  pl.BlockSpec((tm, SEG), lambda i: (i, 2)),
                  tab_spec, tab_spec],
        out_specs=[pl.BlockSpec((tm, SEG), lambda i: (i, 0))] * 3,
        out_shape=[out, out, out],
        compiler_params=_cparams(("parallel",), 32),
        name="rope_prep",
    )(z, z, z, cos, sin)


def _attn_kernel(*refs, n_pieces, q_prepped, lam_init):
    lq_ref, gd_ref, q_ref = refs[:3]
    kv_refs = refs[3:3 + 2 * n_pieces]
    o_ref = refs[3 + 2 * n_pieces]
    lq = lq_ref[...]
    lam = (jnp.exp(jnp.sum(lq[0:1] * lq[1:2], axis=-1, keepdims=True))
           - jnp.exp(jnp.sum(lq[2:3] * lq[3:4], axis=-1, keepdims=True)) + lam_init)
    q = q_ref[...]
    tq = q.shape[0]
    if not q_prepped:
        q = q * (DH_A ** -0.5)
    lo = lax.broadcasted_iota(jnp.int32, q.shape, 1) < DH_A
    zero = jnp.zeros_like(q)
    qq = jnp.concatenate([jnp.where(lo, q, zero), jnp.where(lo, zero, q)], axis=0).astype(BF16)
    scores = [_dot_nt(qq, kv_refs[2 * p][...].astype(BF16)) for p in range(n_pieces)]
    m = scores[0].max(axis=-1, keepdims=True)
    for s in scores[1:]:
        m = jnp.maximum(m, s.max(axis=-1, keepdims=True))
    es = [jnp.exp(s - m) for s in scores]
    l = es[0].sum(axis=-1, keepdims=True)
    for e in es[1:]:
        l = l + e.sum(axis=-1, keepdims=True)
    r = 1.0 / l
    r1 = r[:tq]
    r2 = r[tq:] * lam
    o = None
    for p in range(n_pieces):
        pr = (es[p][:tq] * r1 - es[p][tq:] * r2).astype(BF16)
        t = _dot(pr, kv_refs[2 * p + 1][...].astype(BF16))
        o = t if o is None else o + t
    ms = jnp.mean(o * o, axis=-1, keepdims=True)
    o_ref[...] = o * lax.rsqrt(ms + EPS) * gd_ref[...] * (1.0 - lam_init)


def _attention(q_arr, q_col0, pieces, lambda_qk_l, g_diff_l, n_seq, seqlen, tq, q_prepped, lam_init):
    per_seq = seqlen // tq
    in_specs = [
        pl.BlockSpec((4, DH_A), lambda b, h, i: (0, 0)),
        pl.BlockSpec((1, DV_A), lambda b, h, i: (0, 0)),
        pl.BlockSpec((tq, LANES), lambda b, h, i: (b * per_seq + i, q_col0 + h)),
    ]
    args = [lambda_qk_l, g_diff_l.reshape(1, DV_A), q_arr]
    for k_arr, v_arr, k_spec, v_spec in pieces:
        in_specs += [k_spec, v_spec]
        args += [k_arr, v_arr]
    return pl.pallas_call(
        functools.partial(_attn_kernel, n_pieces=len(pieces), q_prepped=q_prepped, lam_init=lam_init),
        grid=(n_seq, H_A, per_seq),
        in_specs=in_specs,
        out_specs=pl.BlockSpec((tq, LANES), lambda b, h, i: (b * per_seq + i, h)),
        out_shape=jax.ShapeDtypeStruct((n_seq * seqlen, W_A), F32),
        compiler_params=_cparams(("parallel", "parallel", "arbitrary"), 48),
        name="diff_attention",
    )(*args)


def _pool_kernel(u_ref, w_ref, sc_ref, o_ref, *, seqlen):
    rows = u_ref.shape[0]
    pos = lax.broadcasted_iota(jnp.int32, (rows, POOL_CH), 0) % seqlen
    for g, win in enumerate(POOL_WINDOWS):
        sl = slice(g * POOL_CH, (g + 1) * POOL_CH)
        u = u_ref[:, sl]
        half = win // 2
        acc = u
        for d in range(-half, half):
            if d == 0:
                continue
            shifted = pltpu.roll(u, (-d) % rows, 0)
            valid = (pos + d >= 0) & (pos + d < seqlen)
            acc = acc + jnp.where(valid, shifted, 0.0)
        cnt = (jnp.minimum(pos + half, seqlen) - jnp.maximum(pos - half, 0)).astype(F32)
        y = acc / cnt - u
        o_ref[:, sl] = _dot(y.astype(BF16), w_ref[g].astype(BF16)) * sc_ref[:, sl]


def _pool(z, w_pool_l, pool_scale_l, seqlen, rows):
    t = z.shape[0]
    col_block = POOL_COL // W_C
    assert col_block * W_C == POOL_COL
    return pl.pallas_call(
        functools.partial(_pool_kernel, seqlen=seqlen),
        grid=(t // rows,),
        in_specs=[pl.BlockSpec((rows, W_C), lambda i: (i, col_block)),
                  pl.BlockSpec((N_POOL, POOL_CH, POOL_CH), lambda i: (0, 0, 0)),
                  pl.BlockSpec((1, W_C), lambda i: (0, 0))],
        out_specs=pl.BlockSpec((rows, W_C), lambda i: (i, 0)),
        out_shape=jax.ShapeDtypeStruct((t, W_C), F32),
        compiler_params=_cparams(("parallel",), 48),
        name="pool_mixer",
    )(z, w_pool_l, pool_scale_l.reshape(1, W_C))


def _shift_rows(x, prev_row, next_row, pos, seqlen):
    tm = x.shape[0]
    ridx = lax.broadcasted_iota(jnp.int32, x.shape, 0)
    xm = jnp.where(ridx == 0, prev_row, pltpu.roll(x, 1, 0))
    xp = jnp.where(ridx == tm - 1, next_row, pltpu.roll(x, tm - 1, 0))
    xm = jnp.where(pos == 0, 0.0, xm)
    xp = jnp.where(pos == seqlen - 1, 0.0, xp)
    return xm, xp


def _halo_specs(tm, width, col_block, t):
    nb = t // SUBLANES
    per = tm // SUBLANES
    prev = pl.BlockSpec((SUBLANES, width), lambda i, *_: (jnp.maximum(i * per - 1, 0), col_block))
    nxt = pl.BlockSpec((SUBLANES, width), lambda i, *_: (jnp.minimum((i + 1) * per, nb - 1), col_block))
    return prev, nxt


def _dn_prep_kernel(x_ref, xprev_ref, xnext_ref, gate_ref, cw_ref, gp_ref, qkv_ref, gb_ref, *, seqlen, tm):
    i = pl.program_id(0)
    cw = cw_ref[...]
    cblk = 256
    for cb in range(QKVB_W // cblk):
        sl = slice(cb * cblk, (cb + 1) * cblk)
        x = x_ref[:, sl]
        pos = (i * tm + lax.broadcasted_iota(jnp.int32, x.shape, 0)) % seqlen
        xm, xp = _shift_rows(x, xprev_ref[SUBLANES - 1:SUBLANES, sl], xnext_ref[0:1, sl], pos, seqlen)
        y = _silu(xm * cw[0:1, sl] + x * cw[1:2, sl] + xp * cw[2:3, sl])
        for hh in range(cblk // LANES):
            c0 = cb * cblk + hh * LANES
            yh = y[:, hh * LANES:(hh + 1) * LANES]
            if c0 < 2 * H_B * DK_B:
                yh = yh * lax.rsqrt(jnp.sum(yh * yh, axis=-1, keepdims=True) + EPS)
                if c0 < H_B * DK_B:
                    yh = yh * (DK_B ** -0.5)
            qkv_ref[:, c0:c0 + LANES] = yh
    lane = lax.broadcasted_iota(jnp.int32, (tm, LANES), 1)
    for d in range(2):
        raw = gate_ref[:, d * LANES:(d + 1) * LANES]
        gp = gp_ref[d]
        beta = 1.0 / (1.0 + jnp.exp(-raw))
        xa = raw + gp[1:2]
        softplus = jnp.maximum(xa, 0.0) + jnp.log1p(jnp.exp(-jnp.abs(xa)))
        gdec = -jnp.exp(gp[0:1]) * softplus
        gb_ref[d] = jnp.where(lane < H_B, beta, jnp.where(lane < 2 * H_B, gdec, 0.0))


def _dn_prep(z, conv_qkv_l, gate_params, seqlen, tm):
    t = z.shape[0]
    prev, nxt = _halo_specs(tm, QKVB_W, 1, t)
    return pl.pallas_call(
        functools.partial(_dn_prep_kernel, seqlen=seqlen, tm=tm),
        grid=(t // tm,),
        in_specs=[pl.BlockSpec((tm, QKVB_W), lambda i: (i, 1)), prev, nxt,
                  pl.BlockSpec((tm, 2 * LANES), lambda i: (i, GATE_COL // (2 * LANES))),
                  pl.BlockSpec((3, QKVB_W), lambda i: (0, 0)),
                  pl.BlockSpec((2, 2, LANES), lambda i: (0, 0, 0))],
        out_specs=[pl.BlockSpec((tm, QKVB_W), lambda i: (i, 0)),
                   pl.BlockSpec((2, tm, LANES), lambda i: (0, i, 0))],
        out_shape=[jax.ShapeDtypeStruct((t, QKVB_W), F32), jax.ShapeDtypeStruct((2, t, LANES), F32)],
        compiler_params=_cparams(("parallel",), 48),
        name="deltanet_prep",
    )(z, z, z, z, conv_qkv_l, gate_params)


def _dn_scan_kernel(*refs, cs, has_s0):
    if has_s0:
        qkv_ref, gb_ref, s0_ref, o_ref, sfin_ref, s_scr = refs
    else:
        qkv_ref, gb_ref, o_ref, sfin_ref, s_scr = refs
    d = pl.program_id(1)
    st = pl.program_id(2)

    @pl.when(st == 0)
    def _():
        if has_s0:
            s_scr[...] = s0_ref[...]
        else:
            s_scr[...] = jnp.zeros_like(s_scr)

    sgn = jnp.where(d == 0, 1, -1)
    ii = lax.broadcasted_iota(jnp.int32, (CHUNK, CHUNK), 0)
    jj = lax.broadcasted_iota(jnp.int32, (CHUNK, CHUNK), 1)
    rel = (ii - jj) * sgn
    incl = rel >= 0
    strict = rel > 0
    tri = incl.astype(F32)
    eye = (ii == jj).astype(F32)
    eq_masks = [((ii >> (3 + k)) == (jj >> (3 + k))).astype(F32) for k in range(3)] + [jnp.ones((CHUNK, CHUNK), F32)]
    offs = []
    st_q, st_k, st_v, st_beta, st_gc, st_gr, st_gt = [], [], [], [], [], [], []
    for ci in range(cs):
        c = jnp.where(d == 0, ci, cs - 1 - ci)
        off = pl.multiple_of(c * CHUNK, CHUNK)
        offs.append(off)
        gbc = gb_ref[pl.ds(off, CHUNK), :]
        gcol = jnp.dot(tri, gbc, preferred_element_type=F32, precision=lax.Precision.HIGHEST)
        grow = gcol.T
        gtot = jnp.sum(gbc, axis=0, keepdims=True)
        for h in range(H_B):
            st_q.append(qkv_ref[pl.ds(off, CHUNK), h * DK_B:(h + 1) * DK_B])
            st_k.append(qkv_ref[pl.ds(off, CHUNK), (H_B + h) * DK_B:(H_B + h + 1) * DK_B])
            st_v.append(qkv_ref[pl.ds(off, CHUNK), (2 * H_B + h) * DK_B:(2 * H_B + h + 1) * DK_B])
            st_beta.append(gbc[:, h:h + 1])
            st_gc.append(gcol[:, H_B + h:H_B + h + 1])
            st_gr.append(grow[H_B + h:H_B + h + 1, :])
            st_gt.append(gtot[:, H_B + h:H_B + h + 1])
    rng = range(cs * H_B)
    st_eg = [jnp.exp(st_gc[i]) for i in rng]
    st_kq = [_dot_nt(jnp.concatenate([st_k[i], st_q[i]], axis=0).astype(BF16), st_k[i].astype(BF16)) for i in rng]
    st_lmat, st_a, st_x0 = [], [], []
    for i in rng:
        dec = jnp.where(incl, jnp.exp(jnp.where(incl, st_gc[i] - st_gr[i], 0.0)), 0.0)
        st_lmat.append(jnp.where(strict, st_kq[i][:CHUNK] * dec, 0.0) * st_beta[i])
        st_a.append((st_kq[i][CHUNK:] * dec).astype(BF16))
        st_x0.append(jnp.concatenate([st_v[i] * st_beta[i], st_k[i] * (st_beta[i] * st_eg[i])], axis=1))
    st_lbf = [st_lmat[i] * eq_masks[0] for i in rng]
    st_lb = [st_lbf[i].astype(BF16) for i in rng]
    st_p2f = [_dot(st_lb[i], st_lb[i]) for i in rng]
    st_p2 = [st_p2f[i].astype(BF16) for i in rng]
    st_p3 = [_dot(st_lb[i], st_p2[i]) for i in rng]
    st_p4 = [_dot(st_p2[i], st_p2[i]).astype(BF16) for i in rng]
    st_t = [eye - st_lbf[i] + st_p2f[i] - st_p3[i] for i in rng]
    st_t = [st_t[i] + _dot(st_t[i].astype(BF16), st_p4[i]) for i in rng]
    for lvl in range(3):
        cmask = eq_masks[lvl + 1] - eq_masks[lvl]
        st_t16 = [st_t[i].astype(BF16) for i in rng]
        st_m = [_dot((st_lmat[i] * cmask).astype(BF16), st_t16[i]).astype(BF16) for i in rng]
        st_t = [st_t[i] - _dot(st_t16[i], st_m[i]) for i in rng]
    st_x = [st_x0[i] + _dot((st_t[i] - eye).astype(BF16), st_x0[i].astype(BF16)) for i in rng]
    s_cur = [s_scr[h] for h in range(H_B)]
    for ci in range(cs):
        idx = [ci * H_B + h for h in range(H_B)]
        wq = [jnp.concatenate([st_x[i][:, DV_B:], st_q[i] * st_eg[i]], axis=0).astype(BF16) for i in idx]
        r = [_dot(wq[h], s_cur[h].astype(BF16)) for h in range(H_B)]
        vn16 = [(st_x[i][:, :DV_B] - r[h][:CHUNK]).astype(BF16) for h, i in enumerate(idx)]
        o = [r[h][CHUNK:] + _dot(st_a[i], vn16[h]) for h, i in enumerate(idx)]
        kd16 = [(st_k[i] * jnp.exp(st_gt[i] - st_gc[i])).astype(BF16) for i in idx]
        s_cur = [s_cur[h] * jnp.exp(st_gt[i]) + _dot_tn(kd16[h], vn16[h]) for h, i in enumerate(idx)]
        for h in range(H_B):
            o_ref[pl.ds(offs[ci], CHUNK), h * DV_B:(h + 1) * DV_B] = o[h]
    for h in range(H_B):
        s_scr[h] = s_cur[h]

    @pl.when(st == pl.num_programs(2) - 1)
    def _():
        sfin_ref[...] = s_scr[...]


def _dn_scan(qkv, gb, s0, layer, n_seq, seqlen, cs):
    t = qkv.shape[0]
    rows = cs * CHUNK
    nsteps = seqlen // rows

    def rb(b, d, s):
        return b * nsteps + jnp.where(d == 0, s, nsteps - 1 - s)

    in_specs = [pl.BlockSpec((rows, QKVB_W), lambda b, d, s: (rb(b, d, s), 0)),
                pl.BlockSpec((None, rows, LANES), lambda b, d, s: (d, rb(b, d, s), 0))]
    args = [qkv, gb]
    if s0 is not None:
        in_specs.append(pl.BlockSpec((None, None, None, H_B, DK_B, DV_B), lambda b, d, s: (b, layer, d, 0, 0, 0)))
        args.append(s0)
    return pl.pallas_call(
        functools.partial(_dn_scan_kernel, cs=cs, has_s0=s0 is not None),
        grid=(n_seq, 2, nsteps),
        in_specs=in_specs,
        out_specs=[pl.BlockSpec((None, rows, W_B), lambda b, d, s: (d, rb(b, d, s), 0)),
                   pl.BlockSpec((None, None, H_B, DK_B, DV_B), lambda b, d, s: (b, d, 0, 0, 0))],
        out_shape=[jax.ShapeDtypeStruct((2, t, W_B), F32),
                   jax.ShapeDtypeStruct((n_seq, 2, H_B, DK_B, DV_B), F32)],
        scratch_shapes=[pltpu.VMEM((H_B, DK_B, DV_B), F32)],
        compiler_params=_cparams(("parallel", "parallel", "arbitrary"), 32),
        name="deltanet_scan",
    )(*args)


def _out_proj_kernel(oa_ref, od_ref, z_ref, oc_ref, x_ref, gt_ref, g1_ref, gd_ref, w_ref, o_ref, mix_ref):
    mix_ref[:, 0:W_A] = oa_ref[...].astype(BF16)
    gd = gd_ref[...]
    for h in range(H_B):
        sl = slice(h * DV_B, (h + 1) * DV_B)
        ob = od_ref[0, :, sl] + od_ref[1, :, sl]
        ms = jnp.mean(ob * ob, axis=-1, keepdims=True)
        ob = ob * lax.rsqrt(ms + EPS) * gd * _silu(z_ref[:, sl])
        mix_ref[:, W_A + h * DV_B:W_A + (h + 1) * DV_B] = ob.astype(BF16)
    mix_ref[:, W_A + W_B:] = oc_ref[...].astype(BF16)
    y = _dot(mix_ref[...], w_ref[...])
    ms = jnp.mean(y * y, axis=-1, keepdims=True)
    o_ref[...] = x_ref[...] + gt_ref[...] * (y * lax.rsqrt(ms + EPS) * g1_ref[...])


def _out_proj(oa, od, z, oc, x, gains, mod, g_delta_l, w_out_p, row_of_tile, tm):
    t = x.shape[0]
    return pl.pallas_call(
        _out_proj_kernel,
        grid=(t // tm,),
        in_specs=[pl.BlockSpec((tm, W_A), lambda i: (i, 0)),
                  pl.BlockSpec((2, tm, W_B), lambda i: (0, i, 0)),
                  pl.BlockSpec((tm, SEG), lambda i: (i, ZB_SEG)),
                  pl.BlockSpec((tm, W_C), lambda i: (i, 0)),
                  pl.BlockSpec((tm, D_MODEL), lambda i: (i, 0)),
                  _mod_spec(2, row_of_tile),
                  _gain_spec(1),
                  pl.BlockSpec((1, DV_B), lambda i: (0, 0)),
                  pl.BlockSpec((D_MODEL, D_MODEL), lambda i: (0, 0))],
        out_specs=pl.BlockSpec((tm, D_MODEL), lambda i: (i, 0)),
        out_shape=jax.ShapeDtypeStruct((t, D_MODEL), F32),
        scratch_shapes=[pltpu.VMEM((tm, D_MODEL), BF16)],
        compiler_params=_cparams(("parallel",), 56),
        name="out_proj",
    )(oa, od, z, oc, x, mod, gains, g_delta_l.reshape(1, DV_B), w_out_p)


def _ffn_kernel(x_ref, xprev_ref, xnext_ref, g2_ref, sc_ref, sh_ref, gt_ref, g3_ref,
                wg_ref, wu_ref, cg_ref, cu_ref, wd_ref, o_ref, h_ref, p_ref, acc_ref, *, seqlen, tm):
    i = pl.program_id(0)
    j = pl.program_id(1)
    hb = SUBLANES

    @pl.when(j == 0)
    def _():
        norm = lambda v: _modulated_norm(v, g2_ref[...], sc_ref[...], sh_ref[...]).astype(BF16)
        h_ref[0:hb, :] = norm(xprev_ref[...])
        rc = 256
        for r in range(tm // rc):
            h_ref[hb + r * rc:hb + (r + 1) * rc, :] = norm(x_ref[r * rc:(r + 1) * rc, :])
        h_ref[hb + tm:, :] = norm(xnext_ref[...])
        acc_ref[...] = jnp.zeros_like(acc_ref)

    tf = wg_ref.shape[1]
    pos = (i * tm + lax.broadcasted_iota(jnp.int32, (tm, tf), 0)) % seqlen
    has_prev = pos != 0
    has_next = pos != seqlen - 1

    def conv_branch(w_ref, c_ref):
        p_ref[...] = _dot(h_ref[...], w_ref[...])
        cw = c_ref[...]
        return (jnp.where(has_prev, p_ref[hb - 1:hb - 1 + tm, :], 0.0) * cw[0:1]
                + p_ref[hb:hb + tm, :] * cw[1:2]
                + jnp.where(has_next, p_ref[hb + 1:hb + 1 + tm, :], 0.0) * cw[2:3])

    gate = conv_branch(wg_ref, cg_ref)
    up = conv_branch(wu_ref, cu_ref)
    acc_ref[...] += _dot((_silu(gate) * up).astype(BF16), wd_ref[...])

    @pl.when(j == pl.num_programs(1) - 1)
    def _():
        f = acc_ref[...]
        ms = jnp.mean(f * f, axis=-1, keepdims=True)
        o_ref[...] = x_ref[...] + gt_ref[...] * (f * lax.rsqrt(ms + EPS) * g3_ref[...])


def _ffn(x, gains, mod, w_up_p, conv_ffn_l, w_down_p, row_of_tile, seqlen, tm, tf):
    t = x.shape[0]
    nf = D_FF // tf
    prev, nxt = _halo_specs(tm, D_MODEL, 0, t)
    return pl.pallas_call(
        functools.partial(_ffn_kernel, seqlen=seqlen, tm=tm),
        grid=(t // tm, nf),
        in_specs=[pl.BlockSpec((tm, D_MODEL), lambda i, j: (i, 0)), prev, nxt,
                  _gain_spec(2), _mod_spec(4, row_of_tile), _mod_spec(3, row_of_tile), _mod_spec(5, row_of_tile),
                  _gain_spec(3),
                  pl.BlockSpec((D_MODEL, tf), lambda i, j: (0, j)),
                  pl.BlockSpec((D_MODEL, tf), lambda i, j: (0, nf + j)),
                  pl.BlockSpec((3, tf), lambda i, j: (0, j)),
                  pl.BlockSpec((3, tf), lambda i, j: (0, nf + j)),
                  pl.BlockSpec((tf, D_MODEL), lambda i, j: (j, 0))],
        out_specs=pl.BlockSpec((tm, D_MODEL), lambda i, j: (i, 0)),
        out_shape=jax.ShapeDtypeStruct((t, D_MODEL), F32),
        scratch_shapes=[pltpu.VMEM((tm + 2 * SUBLANES, D_MODEL), BF16),
                        pltpu.VMEM((tm + 2 * SUBLANES, tf), F32),
                        pltpu.VMEM((tm, D_MODEL), F32)],
        compiler_params=_cparams(("parallel", "arbitrary"), 56),
        name="conv_ffn",
    )(x, x, x, gains, mod, mod, mod, gains, w_up_p, w_up_p, conv_ffn_l, conv_ffn_l, w_down_p)


def _pack_w_in(w_in_l):
    main = w_in_l[:, :6 * SEG]
    zb = w_in_l[:, 6 * SEG:7 * SEG]
    beta = w_in_l[:, 7 * SEG:7 * SEG + 2 * H_B]
    alpha = w_in_l[:, 7 * SEG + 2 * H_B:7 * SEG + 4 * H_B]
    pool = w_in_l[:, 7 * SEG + 4 * H_B:]
    pad = jnp.zeros((D_MODEL, LANES - 2 * H_B), w_in_l.dtype)
    cols = [main, pool]
    for d in range(2):
        cols += [beta[:, d * H_B:(d + 1) * H_B], alpha[:, d * H_B:(d + 1) * H_B], pad]
    cols.append(zb)
    return jnp.concatenate(cols, axis=1).astype(BF16)


def _gate_params(a_log_l, dt_bias_l):
    gp = jnp.zeros((2, 2, LANES), F32)
    gp = gp.at[:, 0, H_B:2 * H_B].set(a_log_l.astype(F32))
    return gp.at[:, 1, H_B:2 * H_B].set(dt_bias_l.astype(F32))


def _layer(x, mod_l, row_of_tile_fn, p, l, n_seq, seqlen, ctx):
    t = x.shape[0]
    gains = p['norm_gains'][l].reshape(4, 1, D_MODEL)
    lam_init = 0.8 - 0.6 * math.exp(-0.3 * l)
    tm_big = 1024
    assert t % tm_big == 0 and (seqlen % tm_big == 0 or tm_big % seqlen == 0)
    z = _in_proj(x, gains, mod_l, p['w_in_p'][l], row_of_tile_fn(tm_big), tm_big)

    if ctx is None:
        kv_spec = lambda col: pl.BlockSpec((seqlen, LANES), lambda b, h, i: (b, col * H_A + h))
        pieces = [(z, z, kv_spec(1), kv_spec(2))]
        oa = _attention(z, 0, pieces, p['lambda_qk'][l], p['g_diff'][l], n_seq, seqlen, min(256, seqlen), False,
                        lam_init)
    else:
        cache_k, cache_v, state_delta = ctx
        past = cache_k.shape[2]
        q_r, k_r, v_r = _rope_prep(z, seqlen)
        lat_spec = pl.BlockSpec((seqlen, LANES), lambda b, h, i: (b, h))
        ck = cache_k.reshape(n_seq, DEPTH, past, H_A * 2 * DH_A)
        cv = cache_v.reshape(n_seq, DEPTH, past, H_A * DV_A)
        c_spec = pl.BlockSpec((None, None, past, LANES), lambda b, h, i: (b, l, 0, h))
        pieces = [(k_r, v_r, lat_spec, lat_spec), (ck, cv, c_spec, c_spec)]
        oa = _attention(q_r, 0, pieces, p['lambda_qk'][l], p['g_diff'][l], n_seq, seqlen, 128, True, lam_init)

    qkv, gb = _dn_prep(z, p['conv_qkv'][l], _gate_params(p['a_log'][l], p['dt_bias'][l]), seqlen, tm_big)
    od, s_fin = _dn_scan(qkv, gb, None if ctx is None else ctx[2], l, n_seq, seqlen, 4)

    oc = _pool(z, p['w_pool'][l], p['pool_scale'][l], seqlen, max(seqlen, 1024))

    tm = 256
    x = _out_proj(oa, od, z, oc, x, gains, mod_l, p['g_delta'][l], p['w_out_p'][l], row_of_tile_fn(tm), tm)
    tm = 512
    x = _ffn(x, gains, mod_l, p['w_up_p'][l], p['conv_ffn'][l], p['w_down_p'][l], row_of_tile_fn(tm), seqlen, tm, 512)
    if ctx is None:
        k_new = z[:, SEG:2 * SEG].reshape(n_seq, seqlen, H_A, 2 * DH_A)
        v_new = z[:, 2 * SEG:3 * SEG].reshape(n_seq, seqlen, H_A, DV_A)
        return x, (k_new, v_new, s_fin)
    return x, None


def kernel(x_prompt, x_sample, cache_k, cache_v, state_delta, c, c_ctx, w_mod, b_mod, norm_gains, w_in, lambda_qk,
           g_diff, conv_qkv, a_log, dt_bias, g_delta, w_pool, pool_scale, w_out, w_up, conv_ffn, w_down):
    batch, seq, _ = x_prompt.shape
    dec_batch, dec_seq, _ = x_sample.shape
    p = {'norm_gains': norm_gains, 'lambda_qk': lambda_qk, 'g_diff': g_diff, 'conv_qkv': conv_qkv, 'a_log': a_log,
         'dt_bias': dt_bias, 'g_delta': g_delta, 'w_pool': w_pool, 'pool_scale': pool_scale, 'conv_ffn': conv_ffn,
         'w_in_p': [_pack_w_in(w_in[l]) for l in range(DEPTH)],
         'w_out_p': w_out.astype(BF16), 'w_up_p': w_up.astype(BF16), 'w_down_p': w_down.astype(BF16)}

    assert 1 + dec_batch <= SUBLANES
    cond8 = jnp.concatenate([c_ctx[None, :], c, jnp.zeros((SUBLANES - 1 - dec_batch, D_MODEL), F32)], axis=0)
    mod = _modulation(cond8, w_mod, b_mod).reshape(DEPTH, SUBLANES, 6, 1, D_MODEL)

    ctx_rows = lambda tm: (lambda i: 0)
    lat_rows = lambda tm: (lambda i: 1 + (i * tm) // dec_seq)

    xp = x_prompt.reshape(batch * seq, D_MODEL)
    ks, vs, ss = [], [], []
    for l in range(DEPTH):
        xp, (k_l, v_l, s_l) = _layer(xp, mod[l], ctx_rows, p, l, batch, seq, None)
        ks.append(k_l)
        vs.append(v_l)
        ss.append(s_l)
    xs = x_sample.reshape(dec_batch * dec_seq, D_MODEL)
    for l in range(DEPTH):
        xs, _ = _layer(xs, mod[l], lat_rows, p, l, dec_batch, dec_seq, (cache_k, cache_v, state_delta))
    return (xp.reshape(batch, seq, D_MODEL), xs.reshape(dec_batch, dec_seq, D_MODEL),
            jnp.stack(ks, axis=1), jnp.stack(vs, axis=1), jnp.stack(ss, axis=1))
```

```python
import functools
import math

import jax
import jax.numpy as jnp
from jax import lax
from jax.experimental import pallas as pl
from jax.experimental.pallas import tpu as pltpu

D_MODEL = 2048
DEPTH = 2
GRID_W = 64
H_A = 6
DH_A = 64
DV_A = 2 * DH_A
H_B = 6
DK_B = 128
DV_B = 128
CHUNK = 64
N_POOL = 4
POOL_CH = 128
POOL_WINDOWS = (2, 4, 8, 16)
W_A = H_A * DV_A
W_B = H_B * DV_B
W_C = N_POOL * POOL_CH
D_FF = 5632
ROPE_THETA = 10000.0
EPS = 1e-6
Q_SCALE = DH_A ** -0.5 * math.log2(math.e)

LANES = 128
SUBLANES = 8
SEG = 768
QKVB_W = H_B * (2 * DK_B + DV_B)
Z_W = 8 * SEG
N_MAIN_SEG = 7
ZB_SEG = 6
GATE_COL = N_MAIN_SEG * SEG
POOL_COL = GATE_COL + 2 * LANES
N_MOD = 6 * D_MODEL
VMEM_PHYS = 64 * 1024 * 1024

F32 = jnp.float32
BF16 = jnp.bfloat16


def _cparams(sem, vmem_mb):
    assert vmem_mb * 1024 * 1024 < VMEM_PHYS
    return pltpu.CompilerParams(dimension_semantics=sem, vmem_limit_bytes=vmem_mb * 1024 * 1024)


def _silu(x):
    return x * (1.0 / (1.0 + jnp.exp(-x)))


def _dot(a, b):
    return jnp.dot(a, b, preferred_element_type=F32)


def _dot_nt(a, b):
    return lax.dot_general(a, b, (((1,), (1,)), ((), ())), preferred_element_type=F32)


def _dot_tn(a, b):
    return lax.dot_general(a, b, (((0,), (0,)), ((), ())), preferred_element_type=F32)


def _mod_kernel(c_ref, w_ref, b_ref, o_ref):
    a = _silu(c_ref[...]).astype(BF16)
    o_ref[...] = _dot(a, w_ref[...].astype(BF16)) + b_ref[...]


def _modulation(cond8, w_mod, b_mod):
    tn = 1024
    return pl.pallas_call(
        _mod_kernel,
        grid=(DEPTH, N_MOD // tn),
        in_specs=[
            pl.BlockSpec((SUBLANES, D_MODEL), lambda l, j: (0, 0)),
            pl.BlockSpec((None, D_MODEL, tn), lambda l, j: (l, 0, j)),
            pl.BlockSpec((None, 1, tn), lambda l, j: (l, 0, j)),
        ],
        out_specs=pl.BlockSpec((None, SUBLANES, tn), lambda l, j: (l, 0, j)),
        out_shape=jax.ShapeDtypeStruct((DEPTH, SUBLANES, N_MOD), F32),
        compiler_params=_cparams(("parallel", "parallel"), 40),
        name="modulation",
    )(cond8, w_mod, b_mod.reshape(DEPTH, 1, N_MOD))


def _modulated_norm(x, g, sc, sh):
    ms = jnp.mean(x * x, axis=-1, keepdims=True)
    return (x * lax.rsqrt(ms + EPS) * g) * (1.0 + sc) + sh


def _in_proj_kernel(x_ref, g_ref, sc_ref, sh_ref, w_ref, wt_ref, o_ref, h_ref, *, tm):
    j = pl.program_id(1)

    @pl.when(j == 0)
    def _():
        rc = 256
        for r in range(tm // rc):
            x = x_ref[r * rc:(r + 1) * rc, :]
            h_ref[r * rc:(r + 1) * rc, :] = _modulated_norm(x, g_ref[...], sc_ref[...], sh_ref[...]).astype(BF16)

    @pl.when(j < N_MAIN_SEG)
    def _():
        o_ref[...] = _dot(h_ref[...], w_ref[...])

    @pl.when(j == N_MAIN_SEG)
    def _():
        o_ref[...] = _dot(h_ref[...], wt_ref[...])


def _mod_spec(which, row_of_tile):
    return pl.BlockSpec((None, None, 1, D_MODEL), lambda i, *_: (row_of_tile(i), which, 0, 0))


def _gain_spec(k):
    return pl.BlockSpec((None, 1, D_MODEL), lambda i, *_: (k, 0, 0))


def _in_proj(x, gains, mod, w_in16, w_tail, layer, row_of_tile, tm):
    t = x.shape[0]
    return pl.pallas_call(
        functools.partial(_in_proj_kernel, tm=tm),
        grid=(t // tm, Z_W // SEG),
        in_specs=[
            pl.BlockSpec((tm, D_MODEL), lambda i, j: (i, 0)),
            _gain_spec(0),
            _mod_spec(1, row_of_tile),
            _mod_spec(0, row_of_tile),
            pl.BlockSpec((None, D_MODEL, SEG), lambda i, j: (layer, 0, jnp.minimum(j, N_MAIN_SEG - 1))),
            pl.BlockSpec((None, D_MODEL, SEG), lambda i, j: (layer, 0, 0)),
        ],
        out_specs=pl.BlockSpec((tm, SEG), lambda i, j: (i, j)),
        out_shape=jax.ShapeDtypeStruct((t, Z_W), F32),
        scratch_shapes=[pltpu.VMEM((tm, D_MODEL), BF16)],
        compiler_params=_cparams(("parallel", "arbitrary"), 48),
        name="in_proj",
    )(x, gains, mod, mod, w_in16, w_tail)


def _rope_tables(n):
    rows = n // GRID_W
    row = jnp.repeat(jnp.arange(rows), GRID_W)
    col = jnp.arange(rows * GRID_W) % GRID_W
    n_pair = DH_A // 4
    inv = ROPE_THETA ** (-jnp.arange(n_pair, dtype=F32) / n_pair)
    ang = jnp.concatenate([row[:, None] * inv, col[:, None] * inv], axis=-1)
    cos = jnp.repeat(jnp.cos(ang), 2, axis=-1)
    sin = jnp.repeat(jnp.sin(ang), 2, axis=-1) * jnp.tile(jnp.array([-1.0, 1.0], F32), DH_A // 2)
    return jnp.tile(cos, (1, 2)), jnp.tile(sin, (1, 2))


def _rope_kernel(q_ref, k_ref, v_ref, cos_ref, sin_ref, qo_ref, ko_ref, vo_ref):
    cos = cos_ref[...]
    sin = sin_ref[...]
    even = (lax.broadcasted_iota(jnp.int32, cos.shape, 1) % 2) == 0
    for h in range(H_A):
        sl = slice(h * LANES, (h + 1) * LANES)
        for src, dst, scale in ((q_ref, qo_ref, Q_SCALE), (k_ref, ko_ref, None)):
            x = src[:, sl]
            swapped = jnp.where(even, pltpu.roll(x, LANES - 1, 1), pltpu.roll(x, 1, 1))
            y = x * cos + swapped * sin
            if scale is not None:
                y = y * scale
            dst[:, sl] = y.astype(BF16)
        vo_ref[sl, :] = v_ref[:, sl].T.astype(BF16)


def _rope_prep(z, seqlen):
    t = z.shape[0]
    tm = 512
    cos, sin = _rope_tables(seqlen)
    per_seq = seqlen // tm
    tab_spec = pl.BlockSpec((tm, LANES), lambda i: (i % per_seq, 0))
    out = jax.ShapeDtypeStruct((t, SEG), BF16)
    return pl.pallas_call(
        _rope_kernel,
        grid=(t // tm,),
        in_specs=[pl.BlockSpec((tm, SEG), lambda i: (i, 0)),
                  pl.BlockSpec((tm, SEG), lambda i: (i, 1)),
                  pl.BlockSpec((tm, SEG), lambda i: (i, 2)),
                  tab_spec, tab_spec],
        out_specs=[pl.BlockSpec((tm, SEG), lambda i: (i, 0)), pl.BlockSpec((tm, SEG), lambda i: (i, 0)),
                   pl.BlockSpec((SEG, tm), lambda i: (0, i))],
        out_shape=[out, out, jax.ShapeDtypeStruct((SEG, t), BF16)],
        compiler_params=_cparams(("parallel",), 32),
        name="rope_prep",
    )(z, z, z, cos, sin)


def _attn_kernel(*refs, n_pieces, q_prepped, v_transposed, lam_init, sub, hps):
    lq_ref, gd_ref, q_ref = refs[:3]
    kv_refs = refs[3:3 + 2 * n_pieces]
    o_ref = refs[3 + 2 * n_pieces]
    lq = lq_ref[...]
    lam = (jnp.exp(jnp.sum(lq[0:1] * lq[1:2], axis=-1, keepdims=True))
           - jnp.exp(jnp.sum(lq[2:3] * lq[3:4], axis=-1, keepdims=True)) + lam_init)
    lo = lax.broadcasted_iota(jnp.int32, (sub, LANES), 1) < DH_A
    ones_rows = 2 * SUBLANES

    def head_cols(hh):
        return slice(hh * LANES, (hh + 1) * LANES)

    def v_aug(hh, p):
        v_ref = kv_refs[2 * p + 1]
        vt = v_ref[head_cols(hh), :] if v_transposed[p] else v_ref[:, head_cols(hh)].astype(F32).T
        vt = vt.astype(BF16)
        return jnp.concatenate([vt, jnp.ones((ones_rows, vt.shape[1]), BF16)], axis=0)

    def qk(u):
        hh, r = u
        q = q_ref[r * sub:(r + 1) * sub, head_cols(hh)]
        if not q_prepped:
            q = q * Q_SCALE
        zero = jnp.zeros_like(q)
        qq = jnp.concatenate([jnp.where(lo, q, zero), jnp.where(lo, zero, q)], axis=0).astype(BF16)
        return [_dot_nt(kv_refs[2 * p][:, head_cols(hh)].astype(BF16), qq) for p in range(n_pieces)]

    def finish(u, scores, vts):
        hh, r = u
        m = scores[0].max(axis=0, keepdims=True)
        for s in scores[1:]:
            m = jnp.maximum(m, s.max(axis=0, keepdims=True))
        acc = None
        for p in range(n_pieces):
            e = jnp.exp2((scores[p] - m).astype(BF16))
            t = _dot(vts[p], e)
            acc = t if acc is None else acc + t
        l = acc[DV_A:DV_A + 1, :]
        ot = acc[:DV_A, :sub] * (1.0 / l[:, :sub]) - acc[:DV_A, sub:] * (lam / l[:, sub:])
        o = ot.T
        ms = jnp.mean(o * o, axis=-1, keepdims=True)
        o_ref[r * sub:(r + 1) * sub, head_cols(hh)] = o * lax.rsqrt(ms + EPS) * gd_ref[...] * (1.0 - lam_init)

    nsub = q_ref.shape[0] // sub
    units = [(hh, r) for hh in range(hps) for r in range(nsub)]
    nxt = qk(units[0])
    vts = None
    for n, u in enumerate(units):
        cur = nxt
        if u[1] == 0:
            vts = [v_aug(u[0], p) for p in range(n_pieces)]
        if n + 1 < len(units):
            nxt = qk(units[n + 1])
        finish(u, cur, vts)


def _attention(q_arr, pieces, v_transposed, lambda_qk_l, g_diff_l, n_seq, seqlen, tq, hps, q_prepped, lam_init):
    per_seq = seqlen // tq
    width = hps * LANES
    in_specs = [
        pl.BlockSpec((4, DH_A), lambda b, h, i: (0, 0)),
        pl.BlockSpec((1, DV_A), lambda b, h, i: (0, 0)),
        pl.BlockSpec((tq, width), lambda b, h, i: (b * per_seq + i, h)),
    ]
    args = [lambda_qk_l, g_diff_l.reshape(1, DV_A), q_arr]
    for k_arr, v_arr, k_spec, v_spec in pieces:
        in_specs += [k_spec, v_spec]
        args += [k_arr, v_arr]
    return pl.pallas_call(
        functools.partial(_attn_kernel, n_pieces=len(pieces), q_prepped=q_prepped, v_transposed=v_transposed,
                          lam_init=lam_init, sub=128, hps=hps),
        grid=(n_seq, H_A // hps, per_seq),
        in_specs=in_specs,
        out_specs=pl.BlockSpec((tq, width), lambda b, h, i: (b * per_seq + i, h)),
        out_shape=jax.ShapeDtypeStruct((n_seq * seqlen, W_A), F32),
        compiler_params=_cparams(("parallel", "parallel", "arbitrary"), 48),
        name="diff_attention",
    )(*args)


def _pool_kernel(u_ref, w_ref, sc_ref, o_ref, *, seqlen):
    rows = u_ref.shape[0]
    pos = lax.broadcasted_iota(jnp.int32, (rows, POOL_CH), 0) % seqlen
    for g, win in enumerate(POOL_WINDOWS):
        sl = slice(g * POOL_CH, (g + 1) * POOL_CH)
        u = u_ref[:, sl]
        half = win // 2
        acc = u
        for d in range(-half, half):
            if d == 0:
                continue
            shifted = pltpu.roll(u, (-d) % rows, 0)
            valid = (pos + d >= 0) & (pos + d < seqlen)
            acc = acc + jnp.where(valid, shifted, 0.0)
        cnt = (jnp.minimum(pos + half, seqlen) - jnp.maximum(pos - half, 0)).astype(F32)
        y = acc / cnt - u
        o_ref[:, sl] = _dot(y.astype(BF16), w_ref[g].astype(BF16)) * sc_ref[:, sl]


def _pool(z, w_pool_l, pool_scale_l, seqlen, rows):
    t = z.shape[0]
    col_block = POOL_COL // W_C
    assert col_block * W_C == POOL_COL
    return pl.pallas_call(
        functools.partial(_pool_kernel, seqlen=seqlen),
        grid=(t // rows,),
        in_specs=[pl.BlockSpec((rows, W_C), lambda i: (i, col_block)),
                  pl.BlockSpec((N_POOL, POOL_CH, POOL_CH), lambda i: (0, 0, 0)),
                  pl.BlockSpec((1, W_C), lambda i: (0, 0))],
        out_specs=pl.BlockSpec((rows, W_C), lambda i: (i, 0)),
        out_shape=jax.ShapeDtypeStruct((t, W_C), F32),
        compiler_params=_cparams(("parallel",), 48),
        name="pool_mixer",
    )(z, w_pool_l, pool_scale_l.reshape(1, W_C))


def _shift_rows(x, prev_row, next_row, pos, seqlen):
    tm = x.shape[0]
    ridx = lax.broadcasted_iota(jnp.int32, x.shape, 0)
    xm = jnp.where(ridx == 0, prev_row, pltpu.roll(x, 1, 0))
    xp = jnp.where(ridx == tm - 1, next_row, pltpu.roll(x, tm - 1, 0))
    xm = jnp.where(pos == 0, 0.0, xm)
    xp = jnp.where(pos == seqlen - 1, 0.0, xp)
    return xm, xp


def _halo_specs(tm, width, col_block, t):
    nb = t // SUBLANES
    per = tm // SUBLANES
    prev = pl.BlockSpec((SUBLANES, width), lambda i, *_: (jnp.maximum(i * per - 1, 0), col_block))
    nxt = pl.BlockSpec((SUBLANES, width), lambda i, *_: (jnp.minimum((i + 1) * per, nb - 1), col_block))
    return prev, nxt


def _dn_prep_kernel(x_ref, xprev_ref, xnext_ref, gate_ref, cw_ref, gp_ref, qkv_ref, gb_ref, *, seqlen, tm):
    i = pl.program_id(0)
    cw = cw_ref[...]
    cblk = 256
    for cb in range(QKVB_W // cblk):
        sl = slice(cb * cblk, (cb + 1) * cblk)
        x = x_ref[:, sl]
        pos = (i * tm + lax.broadcasted_iota(jnp.int32, x.shape, 0)) % seqlen
        xm, xp = _shift_rows(x, xprev_ref[SUBLANES - 1:SUBLANES, sl], xnext_ref[0:1, sl], pos, seqlen)
        y = _silu(xm * cw[0:1, sl] + x * cw[1:2, sl] + xp * cw[2:3, sl])
        for hh in range(cblk // LANES):
            c0 = cb * cblk + hh * LANES
            yh = y[:, hh * LANES:(hh + 1) * LANES]
            if c0 < 2 * H_B * DK_B:
                yh = yh * lax.rsqrt(jnp.sum(yh * yh, axis=-1, keepdims=True) + EPS)
                if c0 < H_B * DK_B:
                    yh = yh * (DK_B ** -0.5)
            qkv_ref[:, c0:c0 + LANES] = yh
    lane = lax.broadcasted_iota(jnp.int32, (tm, LANES), 1)
    for d in range(2):
        raw = gate_ref[:, d * LANES:(d + 1) * LANES]
        gp = gp_ref[d]
        beta = 1.0 / (1.0 + jnp.exp(-raw))
        xa = raw + gp[1:2]
        softplus = jnp.maximum(xa, 0.0) + jnp.log1p(jnp.exp(-jnp.abs(xa)))
        gdec = -jnp.exp(gp[0:1]) * softplus
        gb_ref[d] = jnp.where(lane < H_B, beta, jnp.where(lane < 2 * H_B, gdec, 0.0))


def _dn_prep(z, conv_qkv_l, gate_params, seqlen, tm):
    t = z.shape[0]
    prev, nxt = _halo_specs(tm, QKVB_W, 1, t)
    return pl.pallas_call(
        functools.partial(_dn_prep_kernel, seqlen=seqlen, tm=tm),
        grid=(t // tm,),
        in_specs=[pl.BlockSpec((tm, QKVB_W), lambda i: (i, 1)), prev, nxt,
                  pl.BlockSpec((tm, 2 * LANES), lambda i: (i, GATE_COL // (2 * LANES))),
                  pl.BlockSpec((3, QKVB_W), lambda i: (0, 0)),
                  pl.BlockSpec((2, 2, LANES), lambda i: (0, 0, 0))],
        out_specs=[pl.BlockSpec((tm, QKVB_W), lambda i: (i, 0)),
                   pl.BlockSpec((2, tm, LANES), lambda i: (0, i, 0))],
        out_shape=[jax.ShapeDtypeStruct((t, QKVB_W), F32), jax.ShapeDtypeStruct((2, t, LANES), F32)],
        compiler_params=_cparams(("parallel",), 48),
        name="deltanet_prep",
    )(z, z, z, z, conv_qkv_l, gate_params)


def _dn_scan_kernel(*refs, cs, has_s0):
    if has_s0:
        qkv_ref, gb_ref, s0_ref, o_ref, sfin_ref, s_scr = refs
    else:
        qkv_ref, gb_ref, o_ref, sfin_ref, s_scr = refs
    d = pl.program_id(1)
    st = pl.program_id(2)

    @pl.when(st == 0)
    def _():
        if has_s0:
            s_scr[...] = s0_ref[...]
        else:
            s_scr[...] = jnp.zeros_like(s_scr)

    sgn = jnp.where(d == 0, 1, -1)
    ii = lax.broadcasted_iota(jnp.int32, (CHUNK, CHUNK), 0)
    jj = lax.broadcasted_iota(jnp.int32, (CHUNK, CHUNK), 1)
    rel = (ii - jj) * sgn
    incl = rel >= 0
    strict = rel > 0
    tri = incl.astype(F32)
    eye = (ii == jj).astype(F32)
    eq_masks = [((ii >> (3 + k)) == (jj >> (3 + k))).astype(F32) for k in range(3)] + [jnp.ones((CHUNK, CHUNK), F32)]
    offs = []
    st_q, st_k, st_v, st_beta, st_gc, st_gr, st_gt = [], [], [], [], [], [], []
    for ci in range(cs):
        c = jnp.where(d == 0, ci, cs - 1 - ci)
        off = pl.multiple_of(c * CHUNK, CHUNK)
        offs.append(off)
        gbc = gb_ref[pl.ds(off, CHUNK), :]
        gcol = jnp.dot(tri, gbc, preferred_element_type=F32, precision=lax.Precision.HIGHEST)
        grow = gcol.T
        gtot = jnp.sum(gbc, axis=0, keepdims=True)
        for h in range(H_B):
            st_q.append(qkv_ref[pl.ds(off, CHUNK), h * DK_B:(h + 1) * DK_B])
            st_k.append(qkv_ref[pl.ds(off, CHUNK), (H_B + h) * DK_B:(H_B + h + 1) * DK_B])
            st_v.append(qkv_ref[pl.ds(off, CHUNK), (2 * H_B + h) * DK_B:(2 * H_B + h + 1) * DK_B])
            st_beta.append(gbc[:, h:h + 1])
            st_gc.append(gcol[:, H_B + h:H_B + h + 1])
            st_gr.append(grow[H_B + h:H_B + h + 1, :])
            st_gt.append(gtot[:, H_B + h:H_B + h + 1])
    rng = range(cs * H_B)
    st_eg = [jnp.exp(st_gc[i]) for i in rng]
    st_kq = [_dot_nt(jnp.concatenate([st_k[i], st_q[i]], axis=0).astype(BF16), st_k[i].astype(BF16)) for i in rng]
    st_lmat, st_a, st_x0 = [], [], []
    for i in rng:
        dec = jnp.where(incl, jnp.exp(jnp.where(incl, st_gc[i] - st_gr[i], 0.0)), 0.0)
        st_lmat.append(jnp.where(strict, st_kq[i][:CHUNK] * dec, 0.0) * st_beta[i])
        st_a.append((st_kq[i][CHUNK:] * dec).astype(BF16))
        st_x0.append(jnp.concatenate([st_v[i] * st_beta[i], st_k[i] * (st_beta[i] * st_eg[i])], axis=1))
    st_lbf = [st_lmat[i] * eq_masks[0] for i in rng]
    st_lb = [st_lbf[i].astype(BF16) for i in rng]
    st_p2f = [_dot(st_lb[i], st_lb[i]) for i in rng]
    st_p2 = [st_p2f[i].astype(BF16) for i in rng]
    st_p3 = [_dot(st_lb[i], st_p2[i]) for i in rng]
    st_p4 = [_dot(st_p2[i], st_p2[i]).astype(BF16) for i in rng]
    st_t = [eye - st_lbf[i] + st_p2f[i] - st_p3[i] for i in rng]
    st_t = [st_t[i] + _dot(st_t[i].astype(BF16), st_p4[i]) for i in rng]
    for lvl in range(3):
        cmask = eq_masks[lvl + 1] - eq_masks[lvl]
        st_t16 = [st_t[i].astype(BF16) for i in rng]
        st_m = [_dot((st_lmat[i] * cmask).astype(BF16), st_t16[i]).astype(BF16) for i in rng]
        st_t = [st_t[i] - _dot(st_t16[i], st_m[i]) for i in rng]
    st_x = [st_x0[i] + _dot((st_t[i] - eye).astype(BF16), st_x0[i].astype(BF16)) for i in rng]
    s_cur = [s_scr[h] for h in range(H_B)]
    for ci in range(cs):
        idx = [ci * H_B + h for h in range(H_B)]
        wq = [jnp.concatenate([st_x[i][:, DV_B:], st_q[i] * st_eg[i]], axis=0).astype(BF16) for i in idx]
        r = [_dot(wq[h], s_cur[h].astype(BF16)) for h in range(H_B)]
        vn16 = [(st_x[i][:, :DV_B] - r[h][:CHUNK]).astype(BF16) for h, i in enumerate(idx)]
        o = [r[h][CHUNK:] + _dot(st_a[i], vn16[h]) for h, i in enumerate(idx)]
        kd16 = [(st_k[i] * jnp.exp(st_gt[i] - st_gc[i])).astype(BF16) for i in idx]
        s_cur = [s_cur[h] * jnp.exp(st_gt[i]) + _dot_tn(kd16[h], vn16[h]) for h, i in enumerate(idx)]
        for h in range(H_B):
            o_ref[pl.ds(offs[ci], CHUNK), h * DV_B:(h + 1) * DV_B] = o[h]
    for h in range(H_B):
        s_scr[h] = s_cur[h]

    @pl.when(st == pl.num_programs(2) - 1)
    def _():
        sfin_ref[...] = s_scr[...]


def _dn_scan(qkv, gb, s0, layer, n_seq, seqlen, cs):
    t = qkv.shape[0]
    rows = cs * CHUNK
    nsteps = seqlen // rows

    def rb(b, d, s):
        return b * nsteps + jnp.where(d == 0, s, nsteps - 1 - s)

    in_specs = [pl.BlockSpec((rows, QKVB_W), lambda b, d, s: (rb(b, d, s), 0)),
                pl.BlockSpec((None, rows, LANES), lambda b, d, s: (d, rb(b, d, s), 0))]
    args = [qkv, gb]
    if s0 is not None:
        in_specs.append(pl.BlockSpec((None, None, None, H_B, DK_B, DV_B), lambda b, d, s: (b, layer, d, 0, 0, 0)))
        args.append(s0)
    return pl.pallas_call(
        functools.partial(_dn_scan_kernel, cs=cs, has_s0=s0 is not None),
        grid=(n_seq, 2, nsteps),
        in_specs=in_specs,
        out_specs=[pl.BlockSpec((None, rows, W_B), lambda b, d, s: (d, rb(b, d, s), 0)),
                   pl.BlockSpec((None, None, H_B, DK_B, DV_B), lambda b, d, s: (b, d, 0, 0, 0))],
        out_shape=[jax.ShapeDtypeStruct((2, t, W_B), F32),
                   jax.ShapeDtypeStruct((n_seq, 2, H_B, DK_B, DV_B), F32)],
        scratch_shapes=[pltpu.VMEM((H_B, DK_B, DV_B), F32)],
        compiler_params=_cparams(("parallel", "parallel", "arbitrary"), 32),
        name="deltanet_scan",
    )(*args)


def _out_proj_kernel(oa_ref, od_ref, z_ref, oc_ref, x_ref, gt_ref, g1_ref, gd_ref, w_ref, o_ref, mix_ref):
    mix_ref[:, 0:W_A] = oa_ref[...].astype(BF16)
    gd = gd_ref[...]
    for h in range(H_B):
        sl = slice(h * DV_B, (h + 1) * DV_B)
        ob = od_ref[0, :, sl] + od_ref[1, :, sl]
        ms = jnp.mean(ob * ob, axis=-1, keepdims=True)
        ob = ob * lax.rsqrt(ms + EPS) * gd * _silu(z_ref[:, sl])
        mix_ref[:, W_A + h * DV_B:W_A + (h + 1) * DV_B] = ob.astype(BF16)
    mix_ref[:, W_A + W_B:] = oc_ref[...].astype(BF16)
    y = _dot(mix_ref[...], w_ref[...])
    ms = jnp.mean(y * y, axis=-1, keepdims=True)
    o_ref[...] = x_ref[...] + gt_ref[...] * (y * lax.rsqrt(ms + EPS) * g1_ref[...])


def _out_proj(oa, od, z, oc, x, gains, mod, g_delta_l, w_out16, layer, row_of_tile, tm):
    t = x.shape[0]
    return pl.pallas_call(
        _out_proj_kernel,
        grid=(t // tm,),
        in_specs=[pl.BlockSpec((tm, W_A), lambda i: (i, 0)),
                  pl.BlockSpec((2, tm, W_B), lambda i: (0, i, 0)),
                  pl.BlockSpec((tm, SEG), lambda i: (i, ZB_SEG)),
                  pl.BlockSpec((tm, W_C), lambda i: (i, 0)),
                  pl.BlockSpec((tm, D_MODEL), lambda i: (i, 0)),
                  _mod_spec(2, row_of_tile),
                  _gain_spec(1),
                  pl.BlockSpec((1, DV_B), lambda i: (0, 0)),
                  pl.BlockSpec((None, D_MODEL, D_MODEL), lambda i: (layer, 0, 0))],
        out_specs=pl.BlockSpec((tm, D_MODEL), lambda i: (i, 0)),
        out_shape=jax.ShapeDtypeStruct((t, D_MODEL), F32),
        scratch_shapes=[pltpu.VMEM((tm, D_MODEL), BF16)],
        compiler_params=_cparams(("parallel",), 56),
        name="out_proj",
    )(oa, od, z, oc, x, mod, gains, g_delta_l.reshape(1, DV_B), w_out16)


def _ffn_kernel(x_ref, xprev_ref, xnext_ref, g2_ref, sc_ref, sh_ref, gt_ref, g3_ref,
                wg_ref, wu_ref, cg_ref, cu_ref, wd_ref, o_ref, h_ref, p_ref, acc_ref, *, seqlen, tm):
    i = pl.program_id(0)
    j = pl.program_id(1)
    hb = SUBLANES

    @pl.when(j == 0)
    def _():
        norm = lambda v: _modulated_norm(v, g2_ref[...], sc_ref[...], sh_ref[...]).astype(BF16)
        h_ref[0:hb, :] = norm(xprev_ref[...])
        rc = 256
        for r in range(tm // rc):
            h_ref[hb + r * rc:hb + (r + 1) * rc, :] = norm(x_ref[r * rc:(r + 1) * rc, :])
        h_ref[hb + tm:, :] = norm(xnext_ref[...])
        acc_ref[...] = jnp.zeros_like(acc_ref)

    tf = wg_ref.shape[1]
    pos = (i * tm + lax.broadcasted_iota(jnp.int32, (tm, tf), 0)) % seqlen
    has_prev = pos != 0
    has_next = pos != seqlen - 1

    def conv_branch(w_ref, c_ref):
        p_ref[...] = _dot(h_ref[...], w_ref[...])
        cw = c_ref[...]
        return (jnp.where(has_prev, p_ref[hb - 1:hb - 1 + tm, :], 0.0) * cw[0:1]
                + p_ref[hb:hb + tm, :] * cw[1:2]
                + jnp.where(has_next, p_ref[hb + 1:hb + 1 + tm, :], 0.0) * cw[2:3])

    gate = conv_branch(wg_ref, cg_ref)
    up = conv_branch(wu_ref, cu_ref)
    acc_ref[...] += _dot((_silu(gate) * up).astype(BF16), wd_ref[...])

    @pl.when(j == pl.num_programs(1) - 1)
    def _():
        f = acc_ref[...]
        ms = jnp.mean(f * f, axis=-1, keepdims=True)
        o_ref[...] = x_ref[...] + gt_ref[...] * (f * lax.rsqrt(ms + EPS) * g3_ref[...])


def _ffn(x, gains, mod, w_up16, conv_ffn, w_down16, layer, row_of_tile, seqlen, tm, tf):
    t = x.shape[0]
    nf = D_FF // tf
    prev, nxt = _halo_specs(tm, D_MODEL, 0, t)
    return pl.pallas_call(
        functools.partial(_ffn_kernel, seqlen=seqlen, tm=tm),
        grid=(t // tm, nf),
        in_specs=[pl.BlockSpec((tm, D_MODEL), lambda i, j: (i, 0)), prev, nxt,
                  _gain_spec(2), _mod_spec(4, row_of_tile), _mod_spec(3, row_of_tile), _mod_spec(5, row_of_tile),
                  _gain_spec(3),
                  pl.BlockSpec((None, D_MODEL, tf), lambda i, j: (layer, 0, j)),
                  pl.BlockSpec((None, D_MODEL, tf), lambda i, j: (layer, 0, nf + j)),
                  pl.BlockSpec((None, 3, tf), lambda i, j: (layer, 0, j)),
                  pl.BlockSpec((None, 3, tf), lambda i, j: (layer, 0, nf + j)),
                  pl.BlockSpec((None, tf, D_MODEL), lambda i, j: (layer, j, 0))],
        out_specs=pl.BlockSpec((tm, D_MODEL), lambda i, j: (i, 0)),
        out_shape=jax.ShapeDtypeStruct((t, D_MODEL), F32),
        scratch_shapes=[pltpu.VMEM((tm + 2 * SUBLANES, D_MODEL), BF16),
                        pltpu.VMEM((tm + 2 * SUBLANES, tf), F32),
                        pltpu.VMEM((tm, D_MODEL), F32)],
        compiler_params=_cparams(("parallel", "arbitrary"), 56),
        name="conv_ffn",
    )(x, x, x, gains, mod, mod, mod, gains, w_up16, w_up16, conv_ffn, conv_ffn, w_down16)


def _pack_w_tail(w_in):
    tail = w_in[:, :, N_MAIN_SEG * SEG:]
    beta = tail[:, :, :2 * H_B]
    alpha = tail[:, :, 2 * H_B:4 * H_B]
    pool = tail[:, :, 4 * H_B:]
    pad = jnp.zeros((DEPTH, D_MODEL, LANES - 2 * H_B), w_in.dtype)
    cols = []
    for d in range(2):
        cols += [beta[:, :, d * H_B:(d + 1) * H_B], alpha[:, :, d * H_B:(d + 1) * H_B], pad]
    cols.append(pool)
    return jnp.concatenate(cols, axis=2).astype(BF16)


def _gate_params(a_log_l, dt_bias_l):
    gp = jnp.zeros((2, 2, LANES), F32)
    gp = gp.at[:, 0, H_B:2 * H_B].set(a_log_l.astype(F32))
    return gp.at[:, 1, H_B:2 * H_B].set(dt_bias_l.astype(F32))


def _layer(x, mod_l, row_of_tile_fn, p, l, n_seq, seqlen, ctx):
    t = x.shape[0]
    gains = p['norm_gains'][l].reshape(4, 1, D_MODEL)
    lam_init = 0.8 - 0.6 * math.exp(-0.3 * l)
    tm_big = 1024
    assert t % tm_big == 0 and (seqlen % tm_big == 0 or tm_big % seqlen == 0)
    z = _in_proj(x, gains, mod_l, p['w_in16'], p['w_tail'], l, row_of_tile_fn(tm_big), tm_big)

    if ctx is None:
        kv_spec = lambda seg: pl.BlockSpec((seqlen, W_A), lambda b, h, i: (b, seg))
        pieces = [(z, z, kv_spec(1), kv_spec(2))]
        oa = _attention(z, pieces, (False,), p['lambda_qk'][l], p['g_diff'][l], n_seq, seqlen, min(256, seqlen), H_A,
                        False, lam_init)
    else:
        cache_k, cache_v, state_delta = ctx
        past = cache_k.shape[2]
        q_r, k_r, v_t = _rope_prep(z, seqlen)
        lat_spec = pl.BlockSpec((seqlen, LANES), lambda b, h, i: (b, h))
        vt_spec = pl.BlockSpec((LANES, seqlen), lambda b, h, i: (h, b))
        ck = cache_k.reshape(n_seq, DEPTH, past, H_A * 2 * DH_A)
        cv = cache_v.reshape(n_seq, DEPTH, past, H_A * DV_A)
        c_spec = pl.BlockSpec((None, None, past, LANES), lambda b, h, i: (b, l, 0, h))
        pieces = [(k_r, v_t, lat_spec, vt_spec), (ck, cv, c_spec, c_spec)]
        oa = _attention(q_r, pieces, (True, False), p['lambda_qk'][l], p['g_diff'][l], n_seq, seqlen, 1024, 1, True,
                        lam_init)

    qkv, gb = _dn_prep(z, p['conv_qkv'][l], _gate_params(p['a_log'][l], p['dt_bias'][l]), seqlen, tm_big)
    od, s_fin = _dn_scan(qkv, gb, None if ctx is None else ctx[2], l, n_seq, seqlen, 4)

    oc = _pool(z, p['w_pool'][l], p['pool_scale'][l], seqlen, max(seqlen, 1024))

    tm = 256
    x = _out_proj(oa, od, z, oc, x, gains, mod_l, p['g_delta'][l], p['w_out16'], l, row_of_tile_fn(tm), tm)
    tm = 512
    x = _ffn(x, gains, mod_l, p['w_up16'], p['conv_ffn'], p['w_down16'], l, row_of_tile_fn(tm), seqlen, tm, 512)
    if ctx is None:
        k_new = z[:, SEG:2 * SEG].reshape(n_seq, seqlen, H_A, 2 * DH_A)
        v_new = z[:, 2 * SEG:3 * SEG].reshape(n_seq, seqlen, H_A, DV_A)
        return x, (k_new, v_new, s_fin)
    return x, None


def kernel(x_prompt, x_sample, cache_k, cache_v, state_delta, c, c_ctx, w_mod, b_mod, norm_gains, w_in, lambda_qk,
           g_diff, conv_qkv, a_log, dt_bias, g_delta, w_pool, pool_scale, w_out, w_up, conv_ffn, w_down):
    batch, seq, _ = x_prompt.shape
    dec_batch, dec_seq, _ = x_sample.shape
    p = {'norm_gains': norm_gains, 'lambda_qk': lambda_qk, 'g_diff': g_diff, 'conv_qkv': conv_qkv, 'a_log': a_log,
         'dt_bias': dt_bias, 'g_delta': g_delta, 'w_pool': w_pool, 'pool_scale': pool_scale, 'conv_ffn': conv_ffn,
         'w_in16': w_in.astype(BF16), 'w_tail': _pack_w_tail(w_in),
         'w_out16': w_out.astype(BF16), 'w_up16': w_up.astype(BF16), 'w_down16': w_down.astype(BF16)}

    assert 1 + dec_batch <= SUBLANES
    cond8 = jnp.concatenate([c_ctx[None, :], c, jnp.zeros((SUBLANES - 1 - dec_batch, D_MODEL), F32)], axis=0)
    mod = _modulation(cond8, w_mod, b_mod).reshape(DEPTH, SUBLANES, 6, 1, D_MODEL)

    ctx_rows = lambda tm: (lambda i: 0)
    lat_rows = lambda tm: (lambda i: 1 + (i * tm) // dec_seq)

    xp = x_prompt.reshape(batch * seq, D_MODEL)
    ks, vs, ss = [], [], []
    for l in range(DEPTH):
        xp, (k_l, v_l, s_l) = _layer(xp, mod[l], ctx_rows, p, l, batch, seq, None)
        ks.append(k_l)
        vs.append(v_l)
        ss.append(s_l)
    xs = x_sample.reshape(dec_batch * dec_seq, D_MODEL)
    for l in range(DEPTH):
        xs, _ = _layer(xs, mod[l], lat_rows, p, l, dec_batch, dec_seq, (cache_k, cache_v, state_delta))
    return (xp.reshape(batch, seq, D_MODEL), xs.reshape(dec_batch, dec_seq, D_MODEL),
            jnp.stack(ks, axis=1), jnp.stack(vs, axis=1), jnp.stack(ss, axis=1))
```

```python
import functools
import math

import jax
import jax.numpy as jnp
from jax import lax
from jax.experimental import pallas as pl
from jax.experimental.pallas import tpu as pltpu

D_MODEL = 2048
DEPTH = 2
GRID_W = 64
H_A = 6
DH_A = 64
DV_A = 2 * DH_A
H_B = 6
DK_B = 128
DV_B = 128
CHUNK = 64
N_POOL = 4
POOL_CH = 128
POOL_WINDOWS = (2, 4, 8, 16)
W_A = H_A * DV_A
W_B = H_B * DV_B
W_C = N_POOL * POOL_CH
D_FF = 5632
ROPE_THETA = 10000.0
EPS = 1e-6
Q_SCALE = DH_A ** -0.5 * math.log2(math.e)

LANES = 128
SUBLANES = 8
SEG = 768
QKVB_W = H_B * (2 * DK_B + DV_B)
Z_W = 8 * SEG
N_MAIN_SEG = 7
ZB_SEG = 6
GATE_COL = N_MAIN_SEG * SEG
POOL_COL = GATE_COL + 2 * LANES
N_MOD = 6 * D_MODEL
VMEM_PHYS = 64 * 1024 * 1024

F32 = jnp.float32
BF16 = jnp.bfloat16


def _cparams(sem, vmem_mb):
    assert vmem_mb * 1024 * 1024 < VMEM_PHYS
    return pltpu.CompilerParams(dimension_semantics=sem, vmem_limit_bytes=vmem_mb * 1024 * 1024)


def _silu(x):
    return x * (1.0 / (1.0 + jnp.exp(-x)))


def _dot(a, b):
    return jnp.dot(a, b, preferred_element_type=F32)


def _dot_nt(a, b):
    return lax.dot_general(a, b, (((1,), (1,)), ((), ())), preferred_element_type=F32)


def _dot_tn(a, b):
    return lax.dot_general(a, b, (((0,), (0,)), ((), ())), preferred_element_type=F32)


def _mod_kernel(c_ref, w_ref, b_ref, o_ref):
    a = _silu(c_ref[...]).astype(BF16)
    o_ref[...] = _dot(a, w_ref[...].astype(BF16)) + b_ref[...]


def _modulation(cond8, w_mod, b_mod):
    tn = 1024
    return pl.pallas_call(
        _mod_kernel,
        grid=(DEPTH, N_MOD // tn),
        in_specs=[
            pl.BlockSpec((SUBLANES, D_MODEL), lambda l, j: (0, 0)),
            pl.BlockSpec((None, D_MODEL, tn), lambda l, j: (l, 0, j)),
            pl.BlockSpec((None, 1, tn), lambda l, j: (l, 0, j)),
        ],
        out_specs=pl.BlockSpec((None, SUBLANES, tn), lambda l, j: (l, 0, j)),
        out_shape=jax.ShapeDtypeStruct((DEPTH, SUBLANES, N_MOD), F32),
        compiler_params=_cparams(("parallel", "parallel"), 40),
        name="modulation",
    )(cond8, w_mod, b_mod.reshape(DEPTH, 1, N_MOD))


def _modulated_norm(x, g, sc, sh):
    ms = jnp.mean(x * x, axis=-1, keepdims=True)
    return (x * lax.rsqrt(ms + EPS) * g) * (1.0 + sc) + sh


def _in_proj_kernel(x_ref, g_ref, sc_ref, sh_ref, w_ref, wt_ref, o_ref, *rest, tm, kv_seqlen):
    h_ref = rest[-1]
    j = pl.program_id(1)

    @pl.when(j == 0)
    def _():
        rc = 256
        for r in range(tm // rc):
            x = x_ref[r * rc:(r + 1) * rc, :]
            h_ref[r * rc:(r + 1) * rc, :] = _modulated_norm(x, g_ref[...], sc_ref[...], sh_ref[...]).astype(BF16)

    @pl.when(j < N_MAIN_SEG)
    def _():
        o_ref[...] = _dot(h_ref[...], w_ref[...])

    @pl.when(j == N_MAIN_SEG)
    def _():
        o_ref[...] = _dot(h_ref[...], wt_ref[...])

    if kv_seqlen is not None:
        for seg, c_ref in ((1, rest[0]), (2, rest[1])):
            @pl.when(j == seg)
            def _():
                for s in range(tm // kv_seqlen):
                    for hh in range(H_A):
                        c_ref[s, hh] = o_ref[s * kv_seqlen:(s + 1) * kv_seqlen, hh * LANES:(hh + 1) * LANES]


def _mod_spec(which, row_of_tile):
    return pl.BlockSpec((None, None, 1, D_MODEL), lambda i, *_: (row_of_tile(i), which, 0, 0))


def _gain_spec(k):
    return pl.BlockSpec((None, 1, D_MODEL), lambda i, *_: (k, 0, 0))


def _in_proj(x, gains, mod, w_in16, w_tail, layer, row_of_tile, tm, kv_seqlen=None):
    t = x.shape[0]
    out_specs = [pl.BlockSpec((tm, SEG), lambda i, j: (i, j))]
    out_shape = [jax.ShapeDtypeStruct((t, Z_W), F32)]
    if kv_seqlen is not None:
        per = tm // kv_seqlen
        out_specs += [pl.BlockSpec((per, H_A, kv_seqlen, LANES), lambda i, j: (i, 0, 0, 0))] * 2
        out_shape += [jax.ShapeDtypeStruct((t // kv_seqlen, H_A, kv_seqlen, LANES), F32)] * 2
    return pl.pallas_call(
        functools.partial(_in_proj_kernel, tm=tm, kv_seqlen=kv_seqlen),
        grid=(t // tm, Z_W // SEG),
        in_specs=[
            pl.BlockSpec((tm, D_MODEL), lambda i, j: (i, 0)),
            _gain_spec(0),
            _mod_spec(1, row_of_tile),
            _mod_spec(0, row_of_tile),
            pl.BlockSpec((None, D_MODEL, SEG), lambda i, j: (layer, 0, jnp.minimum(j, N_MAIN_SEG - 1))),
            pl.BlockSpec((None, D_MODEL, SEG), lambda i, j: (layer, 0, 0)),
        ],
        out_specs=out_specs,
        out_shape=out_shape,
        scratch_shapes=[pltpu.VMEM((tm, D_MODEL), BF16)],
        compiler_params=_cparams(("parallel", "arbitrary"), 48),
        name="in_proj",
    )(x, gains, mod, mod, w_in16, w_tail)


def _rope_tables(n):
    rows = n // GRID_W
    row = jnp.repeat(jnp.arange(rows), GRID_W)
    col = jnp.arange(rows * GRID_W) % GRID_W
    n_pair = DH_A // 4
    inv = ROPE_THETA ** (-jnp.arange(n_pair, dtype=F32) / n_pair)
    ang = jnp.concatenate([row[:, None] * inv, col[:, None] * inv], axis=-1)
    cos = jnp.repeat(jnp.cos(ang), 2, axis=-1)
    sin = jnp.repeat(jnp.sin(ang), 2, axis=-1) * jnp.tile(jnp.array([-1.0, 1.0], F32), DH_A // 2)
    return jnp.tile(cos, (1, 2)), jnp.tile(sin, (1, 2))


def _rope_kernel(q_ref, k_ref, v_ref, cos_ref, sin_ref, qo_ref, ko_ref, vo_ref):
    cos = cos_ref[...]
    sin = sin_ref[...]
    even = (lax.broadcasted_iota(jnp.int32, cos.shape, 1) % 2) == 0
    for h in range(H_A):
        sl = slice(h * LANES, (h + 1) * LANES)
        for src, dst, scale in ((q_ref, qo_ref, Q_SCALE), (k_ref, ko_ref, None)):
            x = src[:, sl]
            swapped = jnp.where(even, pltpu.roll(x, LANES - 1, 1), pltpu.roll(x, 1, 1))
            y = x * cos + swapped * sin
            if scale is not None:
                y = y * scale
            dst[:, sl] = y.astype(BF16)
        vo_ref[sl, :] = v_ref[:, sl].T.astype(BF16)


def _rope_prep(z, seqlen):
    t = z.shape[0]
    tm = 512
    cos, sin = _rope_tables(seqlen)
    per_seq = seqlen // tm
    tab_spec = pl.BlockSpec((tm, LANES), lambda i: (i % per_seq, 0))
    out = jax.ShapeDtypeStruct((t, SEG), BF16)
    return pl.pallas_call(
        _rope_kernel,
        grid=(t // tm,),
        in_specs=[pl.BlockSpec((tm, SEG), lambda i: (i, 0)),
                  pl.BlockSpec((tm, SEG), lambda i: (i, 1)),
                  pl.BlockSpec((tm, SEG), lambda i: (i, 2)),
                  tab_spec, tab_spec],
        out_specs=[pl.BlockSpec((tm, SEG), lambda i: (i, 0)), pl.BlockSpec((tm, SEG), lambda i: (i, 0)),
                   pl.BlockSpec((SEG, tm), lambda i: (0, i))],
        out_shape=[out, out, jax.ShapeDtypeStruct((SEG, t), BF16)],
        compiler_params=_cparams(("parallel",), 32),
        name="rope_prep",
    )(z, z, z, cos, sin)


def _attn_kernel(*refs, n_pieces, q_prepped, v_transposed, lam_init, sub, hps):
    lq_ref, gd_ref, q_ref = refs[:3]
    kv_refs = refs[3:3 + 2 * n_pieces]
    o_ref = refs[3 + 2 * n_pieces]
    lq = lq_ref[...]
    lam = (jnp.exp(jnp.sum(lq[0:1] * lq[1:2], axis=-1, keepdims=True))
           - jnp.exp(jnp.sum(lq[2:3] * lq[3:4], axis=-1, keepdims=True)) + lam_init)
    lo = lax.broadcasted_iota(jnp.int32, (sub, LANES), 1) < DH_A
    ones_rows = 2 * SUBLANES

    def head_cols(hh):
        return slice(hh * LANES, (hh + 1) * LANES)

    def v_aug(hh, p):
        v_ref = kv_refs[2 * p + 1]
        vt = v_ref[head_cols(hh), :] if v_transposed[p] else v_ref[:, head_cols(hh)].astype(F32).T
        vt = vt.astype(BF16)
        return jnp.concatenate([vt, jnp.ones((ones_rows, vt.shape[1]), BF16)], axis=0)

    def qk(u):
        hh, r = u
        q = q_ref[r * sub:(r + 1) * sub, head_cols(hh)]
        if not q_prepped:
            q = q * Q_SCALE
        zero = jnp.zeros_like(q)
        qq = jnp.concatenate([jnp.where(lo, q, zero), jnp.where(lo, zero, q)], axis=0).astype(BF16)
        return [_dot_nt(kv_refs[2 * p][:, head_cols(hh)].astype(BF16), qq) for p in range(n_pieces)]

    def finish(u, scores, vts):
        hh, r = u
        m = scores[0].max(axis=0, keepdims=True)
        for s in scores[1:]:
            m = jnp.maximum(m, s.max(axis=0, keepdims=True))
        acc = None
        for p in range(n_pieces):
            e = jnp.exp2((scores[p] - m).astype(BF16))
            t = _dot(vts[p], e)
            acc = t if acc is None else acc + t
        l = acc[DV_A:DV_A + 1, :]
        ot = acc[:DV_A, :sub] * (1.0 / l[:, :sub]) - acc[:DV_A, sub:] * (lam / l[:, sub:])
        o = ot.T
        ms = jnp.mean(o * o, axis=-1, keepdims=True)
        o_ref[r * sub:(r + 1) * sub, head_cols(hh)] = o * lax.rsqrt(ms + EPS) * gd_ref[...] * (1.0 - lam_init)

    nsub = q_ref.shape[0] // sub
    units = [(hh, r) for hh in range(hps) for r in range(nsub)]
    nxt = qk(units[0])
    vts = None
    for n, u in enumerate(units):
        cur = nxt
        if u[1] == 0:
            vts = [v_aug(u[0], p) for p in range(n_pieces)]
        if n + 1 < len(units):
            nxt = qk(units[n + 1])
        finish(u, cur, vts)


def _attention(q_arr, pieces, v_transposed, lambda_qk_l, g_diff_l, n_seq, seqlen, tq, hps, q_prepped, lam_init):
    per_seq = seqlen // tq
    width = hps * LANES
    in_specs = [
        pl.BlockSpec((4, DH_A), lambda b, h, i: (0, 0)),
        pl.BlockSpec((1, DV_A), lambda b, h, i: (0, 0)),
        pl.BlockSpec((tq, width), lambda b, h, i: (b * per_seq + i, h)),
    ]
    args = [lambda_qk_l, g_diff_l.reshape(1, DV_A), q_arr]
    for k_arr, v_arr, k_spec, v_spec in pieces:
        in_specs += [k_spec, v_spec]
        args += [k_arr, v_arr]
    return pl.pallas_call(
        functools.partial(_attn_kernel, n_pieces=len(pieces), q_prepped=q_prepped, v_transposed=v_transposed,
                          lam_init=lam_init, sub=128, hps=hps),
        grid=(n_seq, H_A // hps, per_seq),
        in_specs=in_specs,
        out_specs=pl.BlockSpec((tq, width), lambda b, h, i: (b * per_seq + i, h)),
        out_shape=jax.ShapeDtypeStruct((n_seq * seqlen, W_A), F32),
        compiler_params=_cparams(("parallel", "parallel", "arbitrary"), 48),
        name="diff_attention",
    )(*args)


def _pool_kernel(u_ref, w_ref, sc_ref, o_ref, *, seqlen):
    rows = u_ref.shape[0]
    pos = lax.broadcasted_iota(jnp.int32, (rows, POOL_CH), 0) % seqlen
    for g, win in enumerate(POOL_WINDOWS):
        sl = slice(g * POOL_CH, (g + 1) * POOL_CH)
        u = u_ref[:, sl]
        half = win // 2
        acc = u
        for d in range(-half, half):
            if d == 0:
                continue
            shifted = pltpu.roll(u, (-d) % rows, 0)
            valid = (pos + d >= 0) & (pos + d < seqlen)
            acc = acc + jnp.where(valid, shifted, 0.0)
        cnt = (jnp.minimum(pos + half, seqlen) - jnp.maximum(pos - half, 0)).astype(F32)
        y = acc / cnt - u
        o_ref[:, sl] = _dot(y.astype(BF16), w_ref[g].astype(BF16)) * sc_ref[:, sl]


def _pool(z, w_pool_l, pool_scale_l, seqlen, rows):
    t = z.shape[0]
    col_block = POOL_COL // W_C
    assert col_block * W_C == POOL_COL
    return pl.pallas_call(
        functools.partial(_pool_kernel, seqlen=seqlen),
        grid=(t // rows,),
        in_specs=[pl.BlockSpec((rows, W_C), lambda i: (i, col_block)),
                  pl.BlockSpec((N_POOL, POOL_CH, POOL_CH), lambda i: (0, 0, 0)),
                  pl.BlockSpec((1, W_C), lambda i: (0, 0))],
        out_specs=pl.BlockSpec((rows, W_C), lambda i: (i, 0)),
        out_shape=jax.ShapeDtypeStruct((t, W_C), F32),
        compiler_params=_cparams(("parallel",), 48),
        name="pool_mixer",
    )(z, w_pool_l, pool_scale_l.reshape(1, W_C))


def _shift_rows(x, prev_row, next_row, pos, seqlen):
    tm = x.shape[0]
    ridx = lax.broadcasted_iota(jnp.int32, x.shape, 0)
    xm = jnp.where(ridx == 0, prev_row, pltpu.roll(x, 1, 0))
    xp = jnp.where(ridx == tm - 1, next_row, pltpu.roll(x, tm - 1, 0))
    xm = jnp.where(pos == 0, 0.0, xm)
    xp = jnp.where(pos == seqlen - 1, 0.0, xp)
    return xm, xp


def _halo_specs(tm, width, col_block, t):
    nb = t // SUBLANES
    per = tm // SUBLANES
    prev = pl.BlockSpec((SUBLANES, width), lambda i, *_: (jnp.maximum(i * per - 1, 0), col_block))
    nxt = pl.BlockSpec((SUBLANES, width), lambda i, *_: (jnp.minimum((i + 1) * per, nb - 1), col_block))
    return prev, nxt


def _dn_prep_kernel(x_ref, xprev_ref, xnext_ref, gate_ref, cw_ref, gp_ref, qkv_ref, gb_ref, *, seqlen, tm):
    i = pl.program_id(0)
    cw = cw_ref[...]
    cblk = 256
    for cb in range(QKVB_W // cblk):
        sl = slice(cb * cblk, (cb + 1) * cblk)
        x = x_ref[:, sl]
        pos = (i * tm + lax.broadcasted_iota(jnp.int32, x.shape, 0)) % seqlen
        xm, xp = _shift_rows(x, xprev_ref[SUBLANES - 1:SUBLANES, sl], xnext_ref[0:1, sl], pos, seqlen)
        y = _silu(xm * cw[0:1, sl] + x * cw[1:2, sl] + xp * cw[2:3, sl])
        for hh in range(cblk // LANES):
            c0 = cb * cblk + hh * LANES
            yh = y[:, hh * LANES:(hh + 1) * LANES]
            if c0 < 2 * H_B * DK_B:
                yh = yh * lax.rsqrt(jnp.sum(yh * yh, axis=-1, keepdims=True) + EPS)
                if c0 < H_B * DK_B:
                    yh = yh * (DK_B ** -0.5)
            qkv_ref[:, c0:c0 + LANES] = yh
    lane = lax.broadcasted_iota(jnp.int32, (tm, LANES), 1)
    for d in range(2):
        raw = gate_ref[:, d * LANES:(d + 1) * LANES]
        gp = gp_ref[d]
        beta = 1.0 / (1.0 + jnp.exp(-raw))
        xa = raw + gp[1:2]
        softplus = jnp.maximum(xa, 0.0) + jnp.log1p(jnp.exp(-jnp.abs(xa)))
        gdec = -jnp.exp(gp[0:1]) * softplus
        gb_ref[d] = jnp.where(lane < H_B, beta, jnp.where(lane < 2 * H_B, gdec, 0.0))


def _dn_prep(z, conv_qkv_l, gate_params, seqlen, tm):
    t = z.shape[0]
    prev, nxt = _halo_specs(tm, QKVB_W, 1, t)
    return pl.pallas_call(
        functools.partial(_dn_prep_kernel, seqlen=seqlen, tm=tm),
        grid=(t // tm,),
        in_specs=[pl.BlockSpec((tm, QKVB_W), lambda i: (i, 1)), prev, nxt,
                  pl.BlockSpec((tm, 2 * LANES), lambda i: (i, GATE_COL // (2 * LANES))),
                  pl.BlockSpec((3, QKVB_W), lambda i: (0, 0)),
                  pl.BlockSpec((2, 2, LANES), lambda i: (0, 0, 0))],
        out_specs=[pl.BlockSpec((tm, QKVB_W), lambda i: (i, 0)),
                   pl.BlockSpec((2, tm, LANES), lambda i: (0, i, 0))],
        out_shape=[jax.ShapeDtypeStruct((t, QKVB_W), F32), jax.ShapeDtypeStruct((2, t, LANES), F32)],
        compiler_params=_cparams(("parallel",), 48),
        name="deltanet_prep",
    )(z, z, z, z, conv_qkv_l, gate_params)


def _dn_scan_kernel(*refs, cs, has_s0):
    if has_s0:
        qkv_refs, gb_refs, s0_ref, o_refs, sfin_ref, s_scr = refs[0:2], refs[2:4], refs[4], refs[5:7], refs[7], refs[8]
    else:
        qkv_refs, gb_refs, o_refs, sfin_ref, s_scr = refs[0:2], refs[2:4], refs[4:6], refs[6], refs[7]
    st = pl.program_id(1)

    @pl.when(st == 0)
    def _():
        if has_s0:
            s_scr[...] = s0_ref[...]
        else:
            s_scr[...] = jnp.zeros_like(s_scr)

    ii = lax.broadcasted_iota(jnp.int32, (CHUNK, CHUNK), 0)
    jj = lax.broadcasted_iota(jnp.int32, (CHUNK, CHUNK), 1)
    incl_d = (ii >= jj, ii <= jj)
    strict_d = (ii > jj, ii < jj)
    eye = (ii == jj).astype(F32)
    eq_masks = [((ii >> (3 + k)) == (jj >> (3 + k))).astype(F32) for k in range(3)] + [jnp.ones((CHUNK, CHUNK), F32)]
    offs = []
    st_q, st_k, st_v, st_beta, st_gc, st_gr, st_gt, st_d = [], [], [], [], [], [], [], []
    for d in range(2):
        tri = incl_d[d].astype(F32)
        for pos in range(cs):
            off = (pos if d == 0 else cs - 1 - pos) * CHUNK
            offs.append(off)
            gbc = gb_refs[d][off:off + CHUNK, :]
            gcol = jnp.dot(tri, gbc, preferred_element_type=F32, precision=lax.Precision.HIGHEST)
            grow = gcol.T
            gtot = jnp.sum(gbc, axis=0, keepdims=True)
            for h in range(H_B):
                st_d.append(d)
                st_q.append(qkv_refs[d][off:off + CHUNK, h * DK_B:(h + 1) * DK_B])
                st_k.append(qkv_refs[d][off:off + CHUNK, (H_B + h) * DK_B:(H_B + h + 1) * DK_B])
                st_v.append(qkv_refs[d][off:off + CHUNK, (2 * H_B + h) * DK_B:(2 * H_B + h + 1) * DK_B])
                st_beta.append(gbc[:, h:h + 1])
                st_gc.append(gcol[:, H_B + h:H_B + h + 1])
                st_gr.append(grow[H_B + h:H_B + h + 1, :])
                st_gt.append(gtot[:, H_B + h:H_B + h + 1])
    rng = range(2 * cs * H_B)
    st_eg = [jnp.exp(st_gc[i]) for i in rng]
    st_kq = [_dot_nt(jnp.concatenate([st_k[i], st_q[i]], axis=0).astype(BF16), st_k[i].astype(BF16)) for i in rng]
    st_lmat, st_a, st_x0 = [], [], []
    for i in rng:
        incl, strict = incl_d[st_d[i]], strict_d[st_d[i]]
        dec = jnp.where(incl, jnp.exp(jnp.where(incl, st_gc[i] - st_gr[i], 0.0)), 0.0)
        st_lmat.append(jnp.where(strict, st_kq[i][:CHUNK] * dec, 0.0) * st_beta[i])
        st_a.append((st_kq[i][CHUNK:] * dec).astype(BF16))
        st_x0.append(jnp.concatenate([st_v[i] * st_beta[i], st_k[i] * (st_beta[i] * st_eg[i])], axis=1))
    st_lbf = [st_lmat[i] * eq_masks[0] for i in rng]
    st_lb = [st_lbf[i].astype(BF16) for i in rng]
    st_p2f = [_dot(st_lb[i], st_lb[i]) for i in rng]
    st_p2 = [st_p2f[i].astype(BF16) for i in rng]
    st_p3 = [_dot(st_lb[i], st_p2[i]) for i in rng]
    st_p4 = [_dot(st_p2[i], st_p2[i]).astype(BF16) for i in rng]
    st_t = [eye - st_lbf[i] + st_p2f[i] - st_p3[i] for i in rng]
    st_t = [st_t[i] + _dot(st_t[i].astype(BF16), st_p4[i]) for i in rng]
    for lvl in range(3):
        cmask = eq_masks[lvl + 1] - eq_masks[lvl]
        st_t16 = [st_t[i].astype(BF16) for i in rng]
        st_m = [_dot((st_lmat[i] * cmask).astype(BF16), st_t16[i]).astype(BF16) for i in rng]
        st_t = [st_t[i] - _dot(st_t16[i], st_m[i]) for i in rng]
    st_x = [st_x0[i] + _dot((st_t[i] - eye).astype(BF16), st_x0[i].astype(BF16)) for i in rng]
    s_cur = [s_scr[d, h] for d in range(2) for h in range(H_B)]
    nch = range(2 * H_B)
    for pos in range(cs):
        idx = [(d * cs + pos) * H_B + h for d in range(2) for h in range(H_B)]
        wq = [jnp.concatenate([st_x[i][:, DV_B:], st_q[i] * st_eg[i]], axis=0).astype(BF16) for i in idx]
        r = [_dot(wq[c], s_cur[c].astype(BF16)) for c in nch]
        vn16 = [(st_x[i][:, :DV_B] - r[c][:CHUNK]).astype(BF16) for c, i in enumerate(idx)]
        o = [r[c][CHUNK:] + _dot(st_a[i], vn16[c]) for c, i in enumerate(idx)]
        kd16 = [(st_k[i] * jnp.exp(st_gt[i] - st_gc[i])).astype(BF16) for i in idx]
        s_cur = [s_cur[c] * jnp.exp(st_gt[i]) + _dot_tn(kd16[c], vn16[c]) for c, i in enumerate(idx)]
        for c in nch:
            d, h = divmod(c, H_B)
            off = offs[d * cs + pos]
            o_refs[d][off:off + CHUNK, h * DV_B:(h + 1) * DV_B] = o[c]
    for c in nch:
        s_scr[c // H_B, c % H_B] = s_cur[c]

    @pl.when(st == pl.num_programs(1) - 1)
    def _():
        sfin_ref[...] = s_scr[...]


def _dn_scan(qkv, gb, s0, layer, n_seq, seqlen, cs):
    t = qkv.shape[0]
    rows = cs * CHUNK
    nsteps = seqlen // rows
    fwd = lambda b, s: b * nsteps + s
    bwd = lambda b, s: b * nsteps + nsteps - 1 - s
    in_specs = [pl.BlockSpec((rows, QKVB_W), lambda b, s: (fwd(b, s), 0)),
                pl.BlockSpec((rows, QKVB_W), lambda b, s: (bwd(b, s), 0)),
                pl.BlockSpec((None, rows, LANES), lambda b, s: (0, fwd(b, s), 0)),
                pl.BlockSpec((None, rows, LANES), lambda b, s: (1, bwd(b, s), 0))]
    args = [qkv, qkv, gb, gb]
    if s0 is not None:
        in_specs.append(pl.BlockSpec((None, None, 2, H_B, DK_B, DV_B), lambda b, s: (b, layer, 0, 0, 0, 0)))
        args.append(s0)
    o_shape = jax.ShapeDtypeStruct((t, W_B), F32)
    return pl.pallas_call(
        functools.partial(_dn_scan_kernel, cs=cs, has_s0=s0 is not None),
        grid=(n_seq, nsteps),
        in_specs=in_specs,
        out_specs=[pl.BlockSpec((rows, W_B), lambda b, s: (fwd(b, s), 0)),
                   pl.BlockSpec((rows, W_B), lambda b, s: (bwd(b, s), 0)),
                   pl.BlockSpec((None, 2, H_B, DK_B, DV_B), lambda b, s: (b, 0, 0, 0, 0))],
        out_shape=[o_shape, o_shape, jax.ShapeDtypeStruct((n_seq, 2, H_B, DK_B, DV_B), F32)],
        scratch_shapes=[pltpu.VMEM((2, H_B, DK_B, DV_B), F32)],
        compiler_params=_cparams(("parallel", "arbitrary"), 40),
        name="deltanet_scan",
    )(*args)


def _out_proj_kernel(oa_ref, of_ref, ob_ref, z_ref, oc_ref, x_ref, gt_ref, g1_ref, gd_ref, w_ref, o_ref, mix_ref):
    mix_ref[:, 0:W_A] = oa_ref[...].astype(BF16)
    gd = gd_ref[...]
    for h in range(H_B):
        sl = slice(h * DV_B, (h + 1) * DV_B)
        ob = of_ref[:, sl] + ob_ref[:, sl]
        ms = jnp.mean(ob * ob, axis=-1, keepdims=True)
        ob = ob * lax.rsqrt(ms + EPS) * gd * _silu(z_ref[:, sl])
        mix_ref[:, W_A + h * DV_B:W_A + (h + 1) * DV_B] = ob.astype(BF16)
    mix_ref[:, W_A + W_B:] = oc_ref[...].astype(BF16)
    y = _dot(mix_ref[...], w_ref[...])
    ms = jnp.mean(y * y, axis=-1, keepdims=True)
    o_ref[...] = x_ref[...] + gt_ref[...] * (y * lax.rsqrt(ms + EPS) * g1_ref[...])


def _out_proj(oa, o_f, o_b, z, oc, x, gains, mod, g_delta_l, w_out16, layer, row_of_tile, tm):
    t = x.shape[0]
    return pl.pallas_call(
        _out_proj_kernel,
        grid=(t // tm,),
        in_specs=[pl.BlockSpec((tm, W_A), lambda i: (i, 0)),
                  pl.BlockSpec((tm, W_B), lambda i: (i, 0)),
                  pl.BlockSpec((tm, W_B), lambda i: (i, 0)),
                  pl.BlockSpec((tm, SEG), lambda i: (i, ZB_SEG)),
                  pl.BlockSpec((tm, W_C), lambda i: (i, 0)),
                  pl.BlockSpec((tm, D_MODEL), lambda i: (i, 0)),
                  _mod_spec(2, row_of_tile),
                  _gain_spec(1),
                  pl.BlockSpec((1, DV_B), lambda i: (0, 0)),
                  pl.BlockSpec((None, D_MODEL, D_MODEL), lambda i: (layer, 0, 0))],
        out_specs=pl.BlockSpec((tm, D_MODEL), lambda i: (i, 0)),
        out_shape=jax.ShapeDtypeStruct((t, D_MODEL), F32),
        scratch_shapes=[pltpu.VMEM((tm, D_MODEL), BF16)],
        compiler_params=_cparams(("parallel",), 56),
        name="out_proj",
    )(oa, o_f, o_b, z, oc, x, mod, gains, g_delta_l.reshape(1, DV_B), w_out16)


def _ffn_kernel(x_ref, xprev_ref, xnext_ref, g2_ref, sc_ref, sh_ref, gt_ref, g3_ref,
                wg_ref, wu_ref, cg_ref, cu_ref, wd_ref, o_ref, h_ref, p_ref, acc_ref, *, seqlen, tm):
    i = pl.program_id(0)
    j = pl.program_id(1)
    hb = SUBLANES

    @pl.when(j == 0)
    def _():
        norm = lambda v: _modulated_norm(v, g2_ref[...], sc_ref[...], sh_ref[...]).astype(BF16)
        h_ref[0:hb, :] = norm(xprev_ref[...])
        rc = 256
        for r in range(tm // rc):
            h_ref[hb + r * rc:hb + (r + 1) * rc, :] = norm(x_ref[r * rc:(r + 1) * rc, :])
        h_ref[hb + tm:, :] = norm(xnext_ref[...])
        acc_ref[...] = jnp.zeros_like(acc_ref)

    tf = wg_ref.shape[1]
    pos = (i * tm + lax.broadcasted_iota(jnp.int32, (tm, tf), 0)) % seqlen
    has_prev = pos != 0
    has_next = pos != seqlen - 1

    def conv_branch(w_ref, c_ref):
        p_ref[...] = _dot(h_ref[...], w_ref[...])
        cw = c_ref[...]
        return (jnp.where(has_prev, p_ref[hb - 1:hb - 1 + tm, :], 0.0) * cw[0:1]
                + p_ref[hb:hb + tm, :] * cw[1:2]
                + jnp.where(has_next, p_ref[hb + 1:hb + 1 + tm, :], 0.0) * cw[2:3])

    gate = conv_branch(wg_ref, cg_ref)
    up = conv_branch(wu_ref, cu_ref)
    acc_ref[...] += _dot((_silu(gate) * up).astype(BF16), wd_ref[...])

    @pl.when(j == pl.num_programs(1) - 1)
    def _():
        f = acc_ref[...]
        ms = jnp.mean(f * f, axis=-1, keepdims=True)
        o_ref[...] = x_ref[...] + gt_ref[...] * (f * lax.rsqrt(ms + EPS) * g3_ref[...])


def _ffn(x, gains, mod, w_up16, conv_ffn, w_down16, layer, row_of_tile, seqlen, tm, tf):
    t = x.shape[0]
    nf = D_FF // tf
    prev, nxt = _halo_specs(tm, D_MODEL, 0, t)
    return pl.pallas_call(
        functools.partial(_ffn_kernel, seqlen=seqlen, tm=tm),
        grid=(t // tm, nf),
        in_specs=[pl.BlockSpec((tm, D_MODEL), lambda i, j: (i, 0)), prev, nxt,
                  _gain_spec(2), _mod_spec(4, row_of_tile), _mod_spec(3, row_of_tile), _mod_spec(5, row_of_tile),
                  _gain_spec(3),
                  pl.BlockSpec((None, D_MODEL, tf), lambda i, j: (layer, 0, j)),
                  pl.BlockSpec((None, D_MODEL, tf), lambda i, j: (layer, 0, nf + j)),
                  pl.BlockSpec((None, 3, tf), lambda i, j: (layer, 0, j)),
                  pl.BlockSpec((None, 3, tf), lambda i, j: (layer, 0, nf + j)),
                  pl.BlockSpec((None, tf, D_MODEL), lambda i, j: (layer, j, 0))],
        out_specs=pl.BlockSpec((tm, D_MODEL), lambda i, j: (i, 0)),
        out_shape=jax.ShapeDtypeStruct((t, D_MODEL), F32),
        scratch_shapes=[pltpu.VMEM((tm + 2 * SUBLANES, D_MODEL), BF16),
                        pltpu.VMEM((tm + 2 * SUBLANES, tf), F32),
                        pltpu.VMEM((tm, D_MODEL), F32)],
        compiler_params=_cparams(("parallel", "arbitrary"), 56),
        name="conv_ffn",
    )(x, x, x, gains, mod, mod, mod, gains, w_up16, w_up16, conv_ffn, conv_ffn, w_down16)


def _pack_w_tail(w_in):
    tail = w_in[:, :, N_MAIN_SEG * SEG:]
    beta = tail[:, :, :2 * H_B]
    alpha = tail[:, :, 2 * H_B:4 * H_B]
    pool = tail[:, :, 4 * H_B:]
    pad = jnp.zeros((DEPTH, D_MODEL, LANES - 2 * H_B), w_in.dtype)
    cols = []
    for d in range(2):
        cols += [beta[:, :, d * H_B:(d + 1) * H_B], alpha[:, :, d * H_B:(d + 1) * H_B], pad]
    cols.append(pool)
    return jnp.concatenate(cols, axis=2).astype(BF16)


def _gate_params(a_log_l, dt_bias_l):
    gp = jnp.zeros((2, 2, LANES), F32)
    gp = gp.at[:, 0, H_B:2 * H_B].set(a_log_l.astype(F32))
    return gp.at[:, 1, H_B:2 * H_B].set(dt_bias_l.astype(F32))


def _layer(x, mod_l, row_of_tile_fn, p, l, n_seq, seqlen, ctx):
    t = x.shape[0]
    gains = p['norm_gains'][l].reshape(4, 1, D_MODEL)
    lam_init = 0.8 - 0.6 * math.exp(-0.3 * l)
    tm_big = 1024
    assert t % tm_big == 0 and (seqlen % tm_big == 0 or tm_big % seqlen == 0)
    if ctx is None:
        tm_in = 512
        z, k_new, v_new = _in_proj(x, gains, mod_l, p['w_in16'], p['w_tail'], l, row_of_tile_fn(tm_in), tm_in, seqlen)
    else:
        z, = _in_proj(x, gains, mod_l, p['w_in16'], p['w_tail'], l, row_of_tile_fn(tm_big), tm_big)

    if ctx is None:
        kv_spec = lambda seg: pl.BlockSpec((seqlen, W_A), lambda b, h, i: (b, seg))
        pieces = [(z, z, kv_spec(1), kv_spec(2))]
        oa = _attention(z, pieces, (False,), p['lambda_qk'][l], p['g_diff'][l], n_seq, seqlen, min(256, seqlen), H_A,
                        False, lam_init)
    else:
        cache_k, cache_v, state_delta = ctx
        past = cache_k.shape[2]
        q_r, k_r, v_t = _rope_prep(z, seqlen)
        lat_spec = pl.BlockSpec((seqlen, LANES), lambda b, h, i: (b, h))
        vt_spec = pl.BlockSpec((LANES, seqlen), lambda b, h, i: (h, b))
        ck = cache_k.transpose(0, 1, 3, 2, 4)
        cv = cache_v.transpose(0, 1, 3, 2, 4)
        c_spec = pl.BlockSpec((None, None, None, past, LANES), lambda b, h, i: (b, l, h, 0, 0))
        pieces = [(k_r, v_t, lat_spec, vt_spec), (ck, cv, c_spec, c_spec)]
        oa = _attention(q_r, pieces, (True, False), p['lambda_qk'][l], p['g_diff'][l], n_seq, seqlen, 1024, 1, True,
                        lam_init)

    qkv, gb = _dn_prep(z, p['conv_qkv'][l], _gate_params(p['a_log'][l], p['dt_bias'][l]), seqlen, tm_big)
    o_f, o_b, s_fin = _dn_scan(qkv, gb, None if ctx is None else ctx[2], l, n_seq, seqlen, 4)

    oc = _pool(z, p['w_pool'][l], p['pool_scale'][l], seqlen, max(seqlen, 1024))

    tm = 256
    x = _out_proj(oa, o_f, o_b, z, oc, x, gains, mod_l, p['g_delta'][l], p['w_out16'], l, row_of_tile_fn(tm), tm)
    tm = 512
    x = _ffn(x, gains, mod_l, p['w_up16'], p['conv_ffn'], p['w_down16'], l, row_of_tile_fn(tm), seqlen, tm, 512)
    if ctx is None:
        return x, (k_new, v_new, s_fin)
    return x, None


def kernel(x_prompt, x_sample, cache_k, cache_v, state_delta, c, c_ctx, w_mod, b_mod, norm_gains, w_in, lambda_qk,
           g_diff, conv_qkv, a_log, dt_bias, g_delta, w_pool, pool_scale, w_out, w_up, conv_ffn, w_down):
    batch, seq, _ = x_prompt.shape
    dec_batch, dec_seq, _ = x_sample.shape
    p = {'norm_gains': norm_gains, 'lambda_qk': lambda_qk, 'g_diff': g_diff, 'conv_qkv': conv_qkv, 'a_log': a_log,
         'dt_bias': dt_bias, 'g_delta': g_delta, 'w_pool': w_pool, 'pool_scale': pool_scale, 'conv_ffn': conv_ffn,
         'w_in16': w_in.astype(BF16), 'w_tail': _pack_w_tail(w_in),
         'w_out16': w_out.astype(BF16), 'w_up16': w_up.astype(BF16), 'w_down16': w_down.astype(BF16)}

    assert 1 + dec_batch <= SUBLANES
    cond8 = jnp.concatenate([c_ctx[None, :], c, jnp.zeros((SUBLANES - 1 - dec_batch, D_MODEL), F32)], axis=0)
    mod = _modulation(cond8, w_mod, b_mod).reshape(DEPTH, SUBLANES, 6, 1, D_MODEL)

    ctx_rows = lambda tm: (lambda i: 0)
    lat_rows = lambda tm: (lambda i: 1 + (i * tm) // dec_seq)

    xp = x_prompt.reshape(batch * seq, D_MODEL)
    ks, vs, ss = [], [], []
    for l in range(DEPTH):
        xp, (k_l, v_l, s_l) = _layer(xp, mod[l], ctx_rows, p, l, batch, seq, None)
        ks.append(k_l)
        vs.append(v_l)
        ss.append(s_l)
    xs = x_sample.reshape(dec_batch * dec_seq, D_MODEL)
    for l in range(DEPTH):
        xs, _ = _layer(xs, mod[l], lat_rows, p, l, dec_batch, dec_seq, (cache_k, cache_v, state_delta))
    return (xp.reshape(batch, seq, D_MODEL), xs.reshape(dec_batch, dec_seq, D_MODEL),
            jnp.stack(ks, axis=1).transpose(0, 1, 3, 2, 4), jnp.stack(vs, axis=1).transpose(0, 1, 3, 2, 4),
            jnp.stack(ss, axis=1))
```

```python
import functools
import math

import jax
import jax.numpy as jnp
from jax import lax
from jax.experimental import pallas as pl
from jax.experimental.pallas import tpu as pltpu

D_MODEL = 2048
DEPTH = 2
GRID_W = 64
H_A = 6
DH_A = 64
DV_A = 2 * DH_A
H_B = 6
DK_B = 128
DV_B = 128
CHUNK = 64
N_POOL = 4
POOL_CH = 128
POOL_WINDOWS = (2, 4, 8, 16)
W_A = H_A * DV_A
W_B = H_B * DV_B
W_C = N_POOL * POOL_CH
D_FF = 5632
ROPE_THETA = 10000.0
EPS = 1e-6
Q_SCALE = DH_A ** -0.5 * math.log2(math.e)

LANES = 128
SUBLANES = 8
SEG = 768
QKVB_W = H_B * (2 * DK_B + DV_B)
Z_W = 8 * SEG
N_MAIN_SEG = 7
ZB_SEG = 6
GATE_COL = N_MAIN_SEG * SEG
POOL_COL = GATE_COL + 2 * LANES
N_MOD = 6 * D_MODEL
VMEM_PHYS = 64 * 1024 * 1024

F32 = jnp.float32
BF16 = jnp.bfloat16


def _cparams(sem, vmem_mb):
    assert vmem_mb * 1024 * 1024 < VMEM_PHYS
    return pltpu.CompilerParams(dimension_semantics=sem, vmem_limit_bytes=vmem_mb * 1024 * 1024)


def _silu(x):
    return x * (1.0 / (1.0 + jnp.exp(-x)))


def _dot(a, b):
    return jnp.dot(a, b, preferred_element_type=F32)


def _dot_nt(a, b):
    return lax.dot_general(a, b, (((1,), (1,)), ((), ())), preferred_element_type=F32)


def _dot_tn(a, b):
    return lax.dot_general(a, b, (((0,), (0,)), ((), ())), preferred_element_type=F32)


def _mod_kernel(c_ref, w_ref, b_ref, o_ref):
    a = _silu(c_ref[...]).astype(BF16)
    o_ref[...] = _dot(a, w_ref[...].astype(BF16)) + b_ref[...]


def _modulation(cond8, w_mod, b_mod):
    tn = 1024
    return pl.pallas_call(
        _mod_kernel,
        grid=(DEPTH, N_MOD // tn),
        in_specs=[
            pl.BlockSpec((SUBLANES, D_MODEL), lambda l, j: (0, 0)),
            pl.BlockSpec((None, D_MODEL, tn), lambda l, j: (l, 0, j)),
            pl.BlockSpec((None, 1, tn), lambda l, j: (l, 0, j)),
        ],
        out_specs=pl.BlockSpec((None, SUBLANES, tn), lambda l, j: (l, 0, j)),
        out_shape=jax.ShapeDtypeStruct((DEPTH, SUBLANES, N_MOD), F32),
        compiler_params=_cparams(("parallel", "parallel"), 40),
        name="modulation",
    )(cond8, w_mod, b_mod.reshape(DEPTH, 1, N_MOD))


NORM_ROWS = 2 * SUBLANES


def _modulated_norm(x, gs, sh):
    ms = jnp.mean(x * x, axis=-1, keepdims=True)
    return x * lax.rsqrt(ms + EPS) * gs + sh


def _row_chunks(n_rows, body):
    for r0 in range(0, n_rows, NORM_ROWS):
        body(r0)


def _norm_residual(y_ref, x_ref, o_ref, gg, n_rows):
    def body(r0):
        rows = pl.ds(r0, NORM_ROWS)
        y = y_ref[rows, :]
        ms = jnp.mean(y * y, axis=-1, keepdims=True)
        o_ref[rows, :] = x_ref[rows, :] + y * lax.rsqrt(ms + EPS) * gg
    _row_chunks(n_rows, body)


def _in_proj_kernel(x_ref, g_ref, sc_ref, sh_ref, w_ref, wt_ref, o_ref, h_ref, *, tm):
    j = pl.program_id(1)

    @pl.when(j == 0)
    def _():
        gs = g_ref[...] * (1.0 + sc_ref[...])
        sh = sh_ref[...]

        def body(r0):
            rows = pl.ds(r0, NORM_ROWS)
            h_ref[rows, :] = _modulated_norm(x_ref[rows, :], gs, sh).astype(BF16)
        _row_chunks(tm, body)

    @pl.when(j < N_MAIN_SEG)
    def _():
        o_ref[...] = _dot(h_ref[...], w_ref[...])

    @pl.when(j == N_MAIN_SEG)
    def _():
        o_ref[...] = _dot(h_ref[...], wt_ref[...])


def _mod_spec(which, row_of_tile):
    return pl.BlockSpec((None, None, 1, D_MODEL), lambda i, *_: (row_of_tile(i), which, 0, 0))


def _gain_spec(k):
    return pl.BlockSpec((None, 1, D_MODEL), lambda i, *_: (k, 0, 0))


def _in_proj(x, gains, mod, w_in16, w_tail, layer, row_of_tile, tm):
    t = x.shape[0]
    return pl.pallas_call(
        functools.partial(_in_proj_kernel, tm=tm),
        grid=(t // tm, Z_W // SEG),
        in_specs=[
            pl.BlockSpec((tm, D_MODEL), lambda i, j: (i, 0)),
            _gain_spec(0),
            _mod_spec(1, row_of_tile),
            _mod_spec(0, row_of_tile),
            pl.BlockSpec((None, D_MODEL, SEG), lambda i, j: (layer, 0, jnp.minimum(j, N_MAIN_SEG - 1))),
            pl.BlockSpec((None, D_MODEL, SEG), lambda i, j: (layer, 0, 0)),
        ],
        out_specs=pl.BlockSpec((tm, SEG), lambda i, j: (i, j)),
        out_shape=jax.ShapeDtypeStruct((t, Z_W), F32),
        scratch_shapes=[pltpu.VMEM((tm, D_MODEL), BF16)],
        compiler_params=_cparams(("parallel", "arbitrary"), 48),
        name="in_proj",
    )(x, gains, mod, mod, w_in16, w_tail)


def _rope_tables(n):
    rows = n // GRID_W
    row = jnp.repeat(jnp.arange(rows), GRID_W)
    col = jnp.arange(rows * GRID_W) % GRID_W
    n_pair = DH_A // 4
    inv = ROPE_THETA ** (-jnp.arange(n_pair, dtype=F32) / n_pair)
    ang = jnp.concatenate([row[:, None] * inv, col[:, None] * inv], axis=-1)
    cos = jnp.repeat(jnp.cos(ang), 2, axis=-1)
    sin = jnp.repeat(jnp.sin(ang), 2, axis=-1) * jnp.tile(jnp.array([-1.0, 1.0], F32), DH_A // 2)
    return jnp.tile(cos, (1, 2)), jnp.tile(sin, (1, 2))


def _rope_kernel(q_ref, k_ref, v_ref, cos_ref, sin_ref, qo_ref, ko_ref, vo_ref):
    cos = cos_ref[...]
    sin = sin_ref[...]
    even = (lax.broadcasted_iota(jnp.int32, cos.shape, 1) % 2) == 0
    for h in range(H_A):
        sl = slice(h * LANES, (h + 1) * LANES)
        for src, dst, scale in ((q_ref, qo_ref, Q_SCALE), (k_ref, ko_ref, None)):
            x = src[:, sl]
            swapped = jnp.where(even, pltpu.roll(x, LANES - 1, 1), pltpu.roll(x, 1, 1))
            y = x * cos + swapped * sin
            if scale is not None:
                y = y * scale
            dst[:, sl] = y.astype(BF16)
        vo_ref[sl, :] = v_ref[:, sl].T.astype(BF16)


def _rope_prep(z, seqlen):
    t = z.shape[0]
    tm = 512
    cos, sin = _rope_tables(seqlen)
    per_seq = seqlen // tm
    tab_spec = pl.BlockSpec((tm, LANES), lambda i: (i % per_seq, 0))
    out = jax.ShapeDtypeStruct((t, SEG), BF16)
    return pl.pallas_call(
        _rope_kernel,
        grid=(t // tm,),
        in_specs=[pl.BlockSpec((tm, SEG), lambda i: (i, 0)),
                  pl.BlockSpec((tm, SEG), lambda i: (i, 1)),
                  pl.BlockSpec((tm, SEG), lambda i: (i, 2)),
                  tab_spec, tab_spec],
        out_specs=[pl.BlockSpec((tm, SEG), lambda i: (i, 0)), pl.BlockSpec((tm, SEG), lambda i: (i, 0)),
                   pl.BlockSpec((SEG, tm), lambda i: (0, i))],
        out_shape=[out, out, jax.ShapeDtypeStruct((SEG, t), BF16)],
        compiler_params=_cparams(("parallel",), 32),
        name="rope_prep",
    )(z, z, z, cos, sin)


def _attn_kernel(*refs, n_pieces, n_alias, cache_slot, q_prepped, v_transposed, lam_init, sub, hps):
    lq_ref, gd_ref, q_ref = refs[:3]
    kv_refs = refs[3:3 + 2 * n_pieces]
    o_ref = refs[3 + 2 * n_pieces + n_alias]
    lq = lq_ref[...]
    lam = (jnp.exp(jnp.sum(lq[0:1] * lq[1:2], axis=-1, keepdims=True))
           - jnp.exp(jnp.sum(lq[2:3] * lq[3:4], axis=-1, keepdims=True)) + lam_init)
    lo = lax.broadcasted_iota(jnp.int32, (sub, LANES), 1) < DH_A
    ones_rows = 2 * SUBLANES

    def head_cols(hh):
        return slice(hh * LANES, (hh + 1) * LANES)

    def v_aug(hh, p):
        v_ref = kv_refs[2 * p + 1]
        vt = v_ref[head_cols(hh), :] if v_transposed[p] else v_ref[:, head_cols(hh)].astype(F32).T
        vt = vt.astype(BF16)
        return jnp.concatenate([vt, jnp.ones((ones_rows, vt.shape[1]), BF16)], axis=0)

    def qk(u):
        hh, r = u
        q = q_ref[r * sub:(r + 1) * sub, head_cols(hh)]
        if not q_prepped:
            q = q * Q_SCALE
        zero = jnp.zeros_like(q)
        qq = jnp.concatenate([jnp.where(lo, q, zero), jnp.where(lo, zero, q)], axis=0).astype(BF16)
        return [_dot_nt(kv_refs[2 * p][:, head_cols(hh)].astype(BF16), qq) for p in range(n_pieces)]

    def finish(u, scores, vts):
        hh, r = u
        m = scores[0].max(axis=0, keepdims=True)
        for s in scores[1:]:
            m = jnp.maximum(m, s.max(axis=0, keepdims=True))
        acc = None
        for p in range(n_pieces):
            e = jnp.exp2((scores[p] - m).astype(BF16))
            t = _dot(vts[p], e)
            acc = t if acc is None else acc + t
        l = acc[DV_A:DV_A + 1, :]
        ot = acc[:DV_A, :sub] * (1.0 / l[:, :sub]) - acc[:DV_A, sub:] * (lam / l[:, sub:])
        o = ot.T
        ms = jnp.mean(o * o, axis=-1, keepdims=True)
        o_ref[r * sub:(r + 1) * sub, head_cols(hh)] = o * lax.rsqrt(ms + EPS) * gd_ref[...] * (1.0 - lam_init)

    if cache_slot is not None:
        for c_ref, src in zip(refs[4 + 2 * n_pieces + n_alias:], kv_refs[:2]):
            dst = c_ref if n_alias else c_ref.at[cache_slot]
            for hh in range(hps):
                dst[hh] = src[:, head_cols(hh)]
            if not n_alias:
                for other in range(c_ref.shape[0]):
                    if other != cache_slot:
                        c_ref[other] = jnp.zeros(c_ref.shape[1:], c_ref.dtype)

    nsub = q_ref.shape[0] // sub
    units = [(hh, r) for hh in range(hps) for r in range(nsub)]
    nxt = qk(units[0])
    vts = None
    for n, u in enumerate(units):
        cur = nxt
        if u[1] == 0:
            vts = [v_aug(u[0], p) for p in range(n_pieces)]
        if n + 1 < len(units):
            nxt = qk(units[n + 1])
        finish(u, cur, vts)


def _attention(q_arr, pieces, v_transposed, lambda_qk_l, g_diff_l, n_seq, seqlen, tq, hps, q_prepped, lam_init,
               cache_out=None):
    per_seq = seqlen // tq
    width = hps * LANES
    in_specs = [
        pl.BlockSpec((4, DH_A), lambda b, h, i: (0, 0)),
        pl.BlockSpec((1, DV_A), lambda b, h, i: (0, 0)),
        pl.BlockSpec((tq, width), lambda b, h, i: (b * per_seq + i, h)),
    ]
    args = [lambda_qk_l, g_diff_l.reshape(1, DV_A), q_arr]
    for k_arr, v_arr, k_spec, v_spec in pieces:
        in_specs += [k_spec, v_spec]
        args += [k_arr, v_arr]
    out_specs = [pl.BlockSpec((tq, width), lambda b, h, i: (b * per_seq + i, h))]
    out_shape = [jax.ShapeDtypeStruct((n_seq * seqlen, W_A), F32)]
    aliases = {}
    n_alias = 0
    if cache_out is not None:
        layer, caches = cache_out
        assert hps == H_A and per_seq == 1
        out_shape += [jax.ShapeDtypeStruct((n_seq, DEPTH, H_A, seqlen, LANES), F32)] * 2
        if caches is None:
            out_specs += [pl.BlockSpec((None, DEPTH, H_A, seqlen, LANES), lambda b, h, i: (b, 0, 0, 0, 0))] * 2
        else:
            out_specs += [pl.BlockSpec((None, None, H_A, seqlen, LANES), lambda b, h, i: (b, layer, 0, 0, 0))] * 2
            n_alias = 2
            aliases = {len(args): 1, len(args) + 1: 2}
            in_specs += [pl.BlockSpec(memory_space=pl.ANY)] * 2
            args += list(caches)
    return pl.pallas_call(
        functools.partial(_attn_kernel, n_pieces=len(pieces), n_alias=n_alias,
                          cache_slot=None if cache_out is None else cache_out[0],
                          q_prepped=q_prepped, v_transposed=v_transposed, lam_init=lam_init, sub=128, hps=hps),
        grid=(n_seq, H_A // hps, per_seq),
        in_specs=in_specs,
        out_specs=out_specs,
        out_shape=out_shape,
        input_output_aliases=aliases,
        compiler_params=_cparams(("parallel", "parallel", "arbitrary"), 48),
        name="diff_attention",
    )(*args)


def _pool_kernel(u_ref, w_ref, sc_ref, o_ref, *, seqlen):
    rows = u_ref.shape[0]
    pos = lax.broadcasted_iota(jnp.int32, (rows, POOL_CH), 0) % seqlen
    for g, win in enumerate(POOL_WINDOWS):
        sl = slice(g * POOL_CH, (g + 1) * POOL_CH)
        u = u_ref[:, sl]
        half = win // 2
        acc = u
        for d in range(-half, half):
            if d == 0:
                continue
            shifted = pltpu.roll(u, (-d) % rows, 0)
            valid = (pos + d >= 0) & (pos + d < seqlen)
            acc = acc + jnp.where(valid, shifted, 0.0)
        cnt = (jnp.minimum(pos + half, seqlen) - jnp.maximum(pos - half, 0)).astype(F32)
        y = acc / cnt - u
        o_ref[:, sl] = _dot(y.astype(BF16), w_ref[g].astype(BF16)) * sc_ref[:, sl]


def _pool(z, w_pool_l, pool_scale_l, seqlen, rows):
    t = z.shape[0]
    col_block = POOL_COL // W_C
    assert col_block * W_C == POOL_COL
    return pl.pallas_call(
        functools.partial(_pool_kernel, seqlen=seqlen),
        grid=(t // rows,),
        in_specs=[pl.BlockSpec((rows, W_C), lambda i: (i, col_block)),
                  pl.BlockSpec((N_POOL, POOL_CH, POOL_CH), lambda i: (0, 0, 0)),
                  pl.BlockSpec((1, W_C), lambda i: (0, 0))],
        out_specs=pl.BlockSpec((rows, W_C), lambda i: (i, 0)),
        out_shape=jax.ShapeDtypeStruct((t, W_C), F32),
        compiler_params=_cparams(("parallel",), 48),
        name="pool_mixer",
    )(z, w_pool_l, pool_scale_l.reshape(1, W_C))


def _shift_rows(x, prev_row, next_row, pos, seqlen):
    tm = x.shape[0]
    ridx = lax.broadcasted_iota(jnp.int32, x.shape, 0)
    xm = jnp.where(ridx == 0, prev_row, pltpu.roll(x, 1, 0))
    xp = jnp.where(ridx == tm - 1, next_row, pltpu.roll(x, tm - 1, 0))
    xm = jnp.where(pos == 0, 0.0, xm)
    xp = jnp.where(pos == seqlen - 1, 0.0, xp)
    return xm, xp


def _halo_specs(tm, width, col_block, t):
    nb = t // SUBLANES
    per = tm // SUBLANES
    prev = pl.BlockSpec((SUBLANES, width), lambda i, *_: (jnp.maximum(i * per - 1, 0), col_block))
    nxt = pl.BlockSpec((SUBLANES, width), lambda i, *_: (jnp.minimum((i + 1) * per, nb - 1), col_block))
    return prev, nxt


def _dn_prep_kernel(x_ref, xprev_ref, xnext_ref, gate_ref, cw_ref, gp_ref, qkv_ref, gb_ref, *, seqlen, tm):
    i = pl.program_id(0)
    cw = cw_ref[...]
    cblk = 256
    for cb in range(QKVB_W // cblk):
        sl = slice(cb * cblk, (cb + 1) * cblk)
        x = x_ref[:, sl]
        pos = (i * tm + lax.broadcasted_iota(jnp.int32, x.shape, 0)) % seqlen
        xm, xp = _shift_rows(x, xprev_ref[SUBLANES - 1:SUBLANES, sl], xnext_ref[0:1, sl], pos, seqlen)
        y = _silu(xm * cw[0:1, sl] + x * cw[1:2, sl] + xp * cw[2:3, sl])
        for hh in range(cblk // LANES):
            c0 = cb * cblk + hh * LANES
            yh = y[:, hh * LANES:(hh + 1) * LANES]
            if c0 < 2 * H_B * DK_B:
                yh = yh * lax.rsqrt(jnp.sum(yh * yh, axis=-1, keepdims=True) + EPS)
                if c0 < H_B * DK_B:
                    yh = yh * (DK_B ** -0.5)
            qkv_ref[:, c0:c0 + LANES] = yh
    lane = lax.broadcasted_iota(jnp.int32, (tm, LANES), 1)
    for d in range(2):
        raw = gate_ref[:, d * LANES:(d + 1) * LANES]
        gp = gp_ref[d]
        beta = 1.0 / (1.0 + jnp.exp(-raw))
        xa = raw + gp[1:2]
        softplus = jnp.maximum(xa, 0.0) + jnp.log1p(jnp.exp(-jnp.abs(xa)))
        gdec = -jnp.exp(gp[0:1]) * softplus
        gb_ref[d] = jnp.where(lane < H_B, beta, jnp.where(lane < 2 * H_B, gdec, 0.0))


def _dn_prep(z, conv_qkv_l, gate_params, seqlen, tm):
    t = z.shape[0]
    prev, nxt = _halo_specs(tm, QKVB_W, 1, t)
    return pl.pallas_call(
        functools.partial(_dn_prep_kernel, seqlen=seqlen, tm=tm),
        grid=(t // tm,),
        in_specs=[pl.BlockSpec((tm, QKVB_W), lambda i: (i, 1)), prev, nxt,
                  pl.BlockSpec((tm, 2 * LANES), lambda i: (i, GATE_COL // (2 * LANES))),
                  pl.BlockSpec((3, QKVB_W), lambda i: (0, 0)),
                  pl.BlockSpec((2, 2, LANES), lambda i: (0, 0, 0))],
        out_specs=[pl.BlockSpec((tm, QKVB_W), lambda i: (i, 0)),
                   pl.BlockSpec((2, tm, LANES), lambda i: (0, i, 0))],
        out_shape=[jax.ShapeDtypeStruct((t, QKVB_W), F32), jax.ShapeDtypeStruct((2, t, LANES), F32)],
        compiler_params=_cparams(("parallel",), 48),
        name="deltanet_prep",
    )(z, z, z, z, conv_qkv_l, gate_params)


def _dn_scan_kernel(*refs, cs, has_s0, state_slot):
    qkv_refs, gb_refs = refs[0:2], refs[2:4]
    s0_ref = refs[4] if has_s0 else None
    o_refs, sfin_ref, s_scr = refs[-4:-2], refs[-2], refs[-1]
    st = pl.program_id(1)

    @pl.when(st == 0)
    def _():
        if has_s0:
            s_scr[...] = s0_ref[...]
        else:
            s_scr[...] = jnp.zeros_like(s_scr)

    ii = lax.broadcasted_iota(jnp.int32, (CHUNK, CHUNK), 0)
    jj = lax.broadcasted_iota(jnp.int32, (CHUNK, CHUNK), 1)
    incl_d = (ii >= jj, ii <= jj)
    strict_d = (ii > jj, ii < jj)
    eye = (ii == jj).astype(F32)
    eq_masks = [((ii >> (3 + k)) == (jj >> (3 + k))).astype(F32) for k in range(3)] + [jnp.ones((CHUNK, CHUNK), F32)]
    offs = []
    st_q, st_k, st_v, st_beta, st_gc, st_gr, st_gt, st_d = [], [], [], [], [], [], [], []
    for d in range(2):
        tri = incl_d[d].astype(F32)
        for pos in range(cs):
            off = (pos if d == 0 else cs - 1 - pos) * CHUNK
            offs.append(off)
            gbc = gb_refs[d][off:off + CHUNK, :]
            gcol = jnp.dot(tri, gbc, preferred_element_type=F32, precision=lax.Precision.HIGHEST)
            grow = gcol.T
            gtot = jnp.sum(gbc, axis=0, keepdims=True)
            for h in range(H_B):
                st_d.append(d)
                st_q.append(qkv_refs[d][off:off + CHUNK, h * DK_B:(h + 1) * DK_B])
                st_k.append(qkv_refs[d][off:off + CHUNK, (H_B + h) * DK_B:(H_B + h + 1) * DK_B])
                st_v.append(qkv_refs[d][off:off + CHUNK, (2 * H_B + h) * DK_B:(2 * H_B + h + 1) * DK_B])
                st_beta.append(gbc[:, h:h + 1])
                st_gc.append(gcol[:, H_B + h:H_B + h + 1])
                st_gr.append(grow[H_B + h:H_B + h + 1, :])
                st_gt.append(gtot[:, H_B + h:H_B + h + 1])
    rng = range(2 * cs * H_B)
    st_eg = [jnp.exp(st_gc[i]) for i in rng]
    st_kq = [_dot_nt(jnp.concatenate([st_k[i], st_q[i]], axis=0).astype(BF16), st_k[i].astype(BF16)) for i in rng]
    st_lmat, st_a, st_x0 = [], [], []
    for i in rng:
        incl, strict = incl_d[st_d[i]], strict_d[st_d[i]]
        dec = jnp.where(incl, jnp.exp(jnp.where(incl, st_gc[i] - st_gr[i], 0.0)), 0.0)
        st_lmat.append(jnp.where(strict, st_kq[i][:CHUNK] * dec, 0.0) * st_beta[i])
        st_a.append((st_kq[i][CHUNK:] * dec).astype(BF16))
        st_x0.append(jnp.concatenate([st_v[i] * st_beta[i], st_k[i] * (st_beta[i] * st_eg[i])], axis=1))
    st_lbf = [st_lmat[i] * eq_masks[0] for i in rng]
    st_lb = [st_lbf[i].astype(BF16) for i in rng]
    st_p2f = [_dot(st_lb[i], st_lb[i]) for i in rng]
    st_p2 = [st_p2f[i].astype(BF16) for i in rng]
    st_p3 = [_dot(st_lb[i], st_p2[i]) for i in rng]
    st_p4 = [_dot(st_p2[i], st_p2[i]).astype(BF16) for i in rng]
    st_t = [eye - st_lbf[i] + st_p2f[i] - st_p3[i] for i in rng]
    st_t = [st_t[i] + _dot(st_t[i].astype(BF16), st_p4[i]) for i in rng]
    for lvl in range(3):
        cmask = eq_masks[lvl + 1] - eq_masks[lvl]
        st_t16 = [st_t[i].astype(BF16) for i in rng]
        st_m = [_dot((st_lmat[i] * cmask).astype(BF16), st_t16[i]).astype(BF16) for i in rng]
        st_t = [st_t[i] - _dot(st_t16[i], st_m[i]) for i in rng]
    st_x = [st_x0[i] + _dot((st_t[i] - eye).astype(BF16), st_x0[i].astype(BF16)) for i in rng]
    s_cur = [s_scr[d, h] for d in range(2) for h in range(H_B)]
    nch = range(2 * H_B)
    for pos in range(cs):
        idx = [(d * cs + pos) * H_B + h for d in range(2) for h in range(H_B)]
        wq = [jnp.concatenate([st_x[i][:, DV_B:], st_q[i] * st_eg[i]], axis=0).astype(BF16) for i in idx]
        r = [_dot(wq[c], s_cur[c].astype(BF16)) for c in nch]
        vn16 = [(st_x[i][:, :DV_B] - r[c][:CHUNK]).astype(BF16) for c, i in enumerate(idx)]
        o = [r[c][CHUNK:] + _dot(st_a[i], vn16[c]) for c, i in enumerate(idx)]
        kd16 = [(st_k[i] * jnp.exp(st_gt[i] - st_gc[i])).astype(BF16) for i in idx]
        s_cur = [s_cur[c] * jnp.exp(st_gt[i]) + _dot_tn(kd16[c], vn16[c]) for c, i in enumerate(idx)]
        for c in nch:
            d, h = divmod(c, H_B)
            off = offs[d * cs + pos]
            o_refs[d][off:off + CHUNK, h * DV_B:(h + 1) * DV_B] = o[c]
    for c in nch:
        s_scr[c // H_B, c % H_B] = s_cur[c]

    @pl.when(st == pl.num_programs(1) - 1)
    def _():
        if state_slot is None:
            sfin_ref[...] = s_scr[...]
        else:
            for other in range(sfin_ref.shape[0]):
                sfin_ref[other] = s_scr[...] if other == state_slot else jnp.zeros_like(s_scr)


def _dn_scan(qkv, gb, s0, layer, n_seq, seqlen, cs, state_out=None):
    t = qkv.shape[0]
    rows = cs * CHUNK
    nsteps = seqlen // rows
    fwd = lambda b, s: b * nsteps + s
    bwd = lambda b, s: b * nsteps + nsteps - 1 - s
    in_specs = [pl.BlockSpec((rows, QKVB_W), lambda b, s: (fwd(b, s), 0)),
                pl.BlockSpec((rows, QKVB_W), lambda b, s: (bwd(b, s), 0)),
                pl.BlockSpec((None, rows, LANES), lambda b, s: (0, fwd(b, s), 0)),
                pl.BlockSpec((None, rows, LANES), lambda b, s: (1, bwd(b, s), 0))]
    args = [qkv, qkv, gb, gb]
    if s0 is not None:
        in_specs.append(pl.BlockSpec((None, None, 2, H_B, DK_B, DV_B), lambda b, s: (b, layer, 0, 0, 0, 0)))
        args.append(s0)
    o_shape = jax.ShapeDtypeStruct((t, W_B), F32)
    aliases = {}
    state_slot = None
    if state_out is None:
        s_spec = pl.BlockSpec((None, 2, H_B, DK_B, DV_B), lambda b, s: (b, 0, 0, 0, 0))
        s_shape = jax.ShapeDtypeStruct((n_seq, 2, H_B, DK_B, DV_B), F32)
    else:
        s_layer, s_prev = state_out
        s_shape = jax.ShapeDtypeStruct((n_seq, DEPTH, 2, H_B, DK_B, DV_B), F32)
        if s_prev is None:
            state_slot = s_layer
            s_spec = pl.BlockSpec((None, DEPTH, 2, H_B, DK_B, DV_B), lambda b, s: (b, 0, 0, 0, 0, 0))
        else:
            s_spec = pl.BlockSpec((None, None, 2, H_B, DK_B, DV_B), lambda b, s: (b, s_layer, 0, 0, 0, 0))
            aliases = {len(args): 2}
            in_specs.append(pl.BlockSpec(memory_space=pl.ANY))
            args.append(s_prev)
    return pl.pallas_call(
        functools.partial(_dn_scan_kernel, cs=cs, has_s0=s0 is not None, state_slot=state_slot),
        grid=(n_seq, nsteps),
        in_specs=in_specs,
        out_specs=[pl.BlockSpec((rows, W_B), lambda b, s: (fwd(b, s), 0)),
                   pl.BlockSpec((rows, W_B), lambda b, s: (bwd(b, s), 0)),
                   s_spec],
        out_shape=[o_shape, o_shape, s_shape],
        input_output_aliases=aliases,
        scratch_shapes=[pltpu.VMEM((2, H_B, DK_B, DV_B), F32)],
        compiler_params=_cparams(("parallel", "arbitrary"), 40),
        name="deltanet_scan",
    )(*args)


def _out_proj_kernel(oa_ref, of_ref, ob_ref, z_ref, oc_ref, x_ref, gt_ref, g1_ref, gd_ref, w_ref, o_ref, mix_ref):
    mix_ref[:, 0:W_A] = oa_ref[...].astype(BF16)
    gd = gd_ref[...]
    for h in range(H_B):
        sl = slice(h * DV_B, (h + 1) * DV_B)
        ob = of_ref[:, sl] + ob_ref[:, sl]
        ms = jnp.mean(ob * ob, axis=-1, keepdims=True)
        ob = ob * lax.rsqrt(ms + EPS) * gd * _silu(z_ref[:, sl])
        mix_ref[:, W_A + h * DV_B:W_A + (h + 1) * DV_B] = ob.astype(BF16)
    mix_ref[:, W_A + W_B:] = oc_ref[...].astype(BF16)
    o_ref[...] = _dot(mix_ref[...], w_ref[...])
    _norm_residual(o_ref, x_ref, o_ref, gt_ref[...] * g1_ref[...], o_ref.shape[0])


def _out_proj(oa, o_f, o_b, z, oc, x, gains, mod, g_delta_l, w_out16, layer, row_of_tile, tm):
    t = x.shape[0]
    return pl.pallas_call(
        _out_proj_kernel,
        grid=(t // tm,),
        in_specs=[pl.BlockSpec((tm, W_A), lambda i: (i, 0)),
                  pl.BlockSpec((tm, W_B), lambda i: (i, 0)),
                  pl.BlockSpec((tm, W_B), lambda i: (i, 0)),
                  pl.BlockSpec((tm, SEG), lambda i: (i, ZB_SEG)),
                  pl.BlockSpec((tm, W_C), lambda i: (i, 0)),
                  pl.BlockSpec((tm, D_MODEL), lambda i: (i, 0)),
                  _mod_spec(2, row_of_tile),
                  _gain_spec(1),
                  pl.BlockSpec((1, DV_B), lambda i: (0, 0)),
                  pl.BlockSpec((None, D_MODEL, D_MODEL), lambda i: (layer, 0, 0))],
        out_specs=pl.BlockSpec((tm, D_MODEL), lambda i: (i, 0)),
        out_shape=jax.ShapeDtypeStruct((t, D_MODEL), F32),
        scratch_shapes=[pltpu.VMEM((tm, D_MODEL), BF16)],
        compiler_params=_cparams(("parallel",), 56),
        name="out_proj",
    )(oa, o_f, o_b, z, oc, x, mod, gains, g_delta_l.reshape(1, DV_B), w_out16)


def _ffn_kernel(x_ref, xprev_ref, xnext_ref, g2_ref, sc_ref, sh_ref, gt_ref, g3_ref,
                wg_ref, wu_ref, cg_ref, cu_ref, wd_ref, o_ref, h_ref, p_ref, acc_ref, *, seqlen, tm):
    i = pl.program_id(0)
    j = pl.program_id(1)
    hb = SUBLANES

    @pl.when(j == 0)
    def _():
        gs = g2_ref[...] * (1.0 + sc_ref[...])
        sh = sh_ref[...]
        norm = lambda v: _modulated_norm(v, gs, sh).astype(BF16)
        h_ref[0:hb, :] = norm(xprev_ref[...])
        for r0 in range(0, tm, NORM_ROWS):
            h_ref[hb + r0:hb + r0 + NORM_ROWS, :] = norm(x_ref[r0:r0 + NORM_ROWS, :])
        h_ref[hb + tm:, :] = norm(xnext_ref[...])
        acc_ref[...] = jnp.zeros_like(acc_ref)

    tf = wg_ref.shape[1]
    pos = (i * tm + lax.broadcasted_iota(jnp.int32, (tm, tf), 0)) % seqlen
    has_prev = pos != 0
    has_next = pos != seqlen - 1

    def conv_branch(w_ref, c_ref):
        p_ref[...] = _dot(h_ref[...], w_ref[...])
        cw = c_ref[...]
        return (jnp.where(has_prev, p_ref[hb - 1:hb - 1 + tm, :], 0.0) * cw[0:1]
                + p_ref[hb:hb + tm, :] * cw[1:2]
                + jnp.where(has_next, p_ref[hb + 1:hb + 1 + tm, :], 0.0) * cw[2:3])

    gate = conv_branch(wg_ref, cg_ref)
    up = conv_branch(wu_ref, cu_ref)
    acc_ref[...] += _dot((_silu(gate) * up).astype(BF16), wd_ref[...])

    @pl.when(j == pl.num_programs(1) - 1)
    def _():
        _norm_residual(acc_ref, x_ref, o_ref, gt_ref[...] * g3_ref[...], tm)


def _ffn(x, gains, mod, w_up16, conv_ffn, w_down16, layer, row_of_tile, seqlen, tm, tf):
    t = x.shape[0]
    nf = D_FF // tf
    prev, nxt = _halo_specs(tm, D_MODEL, 0, t)
    return pl.pallas_call(
        functools.partial(_ffn_kernel, seqlen=seqlen, tm=tm),
        grid=(t // tm, nf),
        in_specs=[pl.BlockSpec((tm, D_MODEL), lambda i, j: (i, 0)), prev, nxt,
                  _gain_spec(2), _mod_spec(4, row_of_tile), _mod_spec(3, row_of_tile), _mod_spec(5, row_of_tile),
                  _gain_spec(3),
                  pl.BlockSpec((None, D_MODEL, tf), lambda i, j: (layer, 0, j)),
                  pl.BlockSpec((None, D_MODEL, tf), lambda i, j: (layer, 0, nf + j)),
                  pl.BlockSpec((None, 3, tf), lambda i, j: (layer, 0, j)),
                  pl.BlockSpec((None, 3, tf), lambda i, j: (layer, 0, nf + j)),
                  pl.BlockSpec((None, tf, D_MODEL), lambda i, j: (layer, j, 0))],
        out_specs=pl.BlockSpec((tm, D_MODEL), lambda i, j: (i, 0)),
        out_shape=jax.ShapeDtypeStruct((t, D_MODEL), F32),
        scratch_shapes=[pltpu.VMEM((tm + 2 * SUBLANES, D_MODEL), BF16),
                        pltpu.VMEM((tm + 2 * SUBLANES, tf), F32),
                        pltpu.VMEM((tm, D_MODEL), F32)],
        compiler_params=_cparams(("parallel", "arbitrary"), 56),
        name="conv_ffn",
    )(x, x, x, gains, mod, mod, mod, gains, w_up16, w_up16, conv_ffn, conv_ffn, w_down16)


def _pack_w_tail(w_in):
    tail = w_in[:, :, N_MAIN_SEG * SEG:]
    beta = tail[:, :, :2 * H_B]
    alpha = tail[:, :, 2 * H_B:4 * H_B]
    pool = tail[:, :, 4 * H_B:]
    pad = jnp.zeros((DEPTH, D_MODEL, LANES - 2 * H_B), w_in.dtype)
    cols = []
    for d in range(2):
        cols += [beta[:, :, d * H_B:(d + 1) * H_B], alpha[:, :, d * H_B:(d + 1) * H_B], pad]
    cols.append(pool)
    return jnp.concatenate(cols, axis=2).astype(BF16)


def _gate_params(a_log_l, dt_bias_l):
    gp = jnp.zeros((2, 2, LANES), F32)
    gp = gp.at[:, 0, H_B:2 * H_B].set(a_log_l.astype(F32))
    return gp.at[:, 1, H_B:2 * H_B].set(dt_bias_l.astype(F32))


def _layer(x, mod_l, row_of_tile_fn, p, l, n_seq, seqlen, ctx, new_caches=None):
    t = x.shape[0]
    gains = p['norm_gains'][l].reshape(4, 1, D_MODEL)
    lam_init = 0.8 - 0.6 * math.exp(-0.3 * l)
    tm_big = 1024
    assert t % tm_big == 0 and (seqlen % tm_big == 0 or tm_big % seqlen == 0)
    z = _in_proj(x, gains, mod_l, p['w_in16'], p['w_tail'], l, row_of_tile_fn(tm_big), tm_big)

    if ctx is None:
        kv_spec = lambda seg: pl.BlockSpec((seqlen, W_A), lambda b, h, i: (b, seg))
        pieces = [(z, z, kv_spec(1), kv_spec(2))]
        kv_prev = None if new_caches is None else new_caches[:2]
        oa, k_new, v_new = _attention(z, pieces, (False,), p['lambda_qk'][l], p['g_diff'][l], n_seq, seqlen, seqlen, H_A,
                                      False, lam_init, cache_out=(l, kv_prev))
    else:
        cache_k, cache_v, state_delta = ctx
        past = cache_k.shape[2]
        q_r, k_r, v_t = _rope_prep(z, seqlen)
        lat_spec = pl.BlockSpec((seqlen, LANES), lambda b, h, i: (b, h))
        vt_spec = pl.BlockSpec((LANES, seqlen), lambda b, h, i: (h, b))
        ck = cache_k.transpose(0, 1, 3, 2, 4)
        cv = cache_v.transpose(0, 1, 3, 2, 4)
        c_spec = pl.BlockSpec((None, None, None, past, LANES), lambda b, h, i: (b, l, h, 0, 0))
        pieces = [(k_r, v_t, lat_spec, vt_spec), (ck, cv, c_spec, c_spec)]
        oa, = _attention(q_r, pieces, (True, False), p['lambda_qk'][l], p['g_diff'][l], n_seq, seqlen, 1024, 1, True,
                         lam_init)

    qkv, gb = _dn_prep(z, p['conv_qkv'][l], _gate_params(p['a_log'][l], p['dt_bias'][l]), seqlen, tm_big)
    s_prev = None if new_caches is None else new_caches[2]
    o_f, o_b, s_fin = _dn_scan(qkv, gb, None if ctx is None else ctx[2], l, n_seq, seqlen, 4,
                               state_out=(l, s_prev) if ctx is None else None)

    oc = _pool(z, p['w_pool'][l], p['pool_scale'][l], seqlen, max(seqlen, 1024))

    tm = 256
    x = _out_proj(oa, o_f, o_b, z, oc, x, gains, mod_l, p['g_delta'][l], p['w_out16'], l, row_of_tile_fn(tm), tm)
    tm = 512
    x = _ffn(x, gains, mod_l, p['w_up16'], p['conv_ffn'], p['w_down16'], l, row_of_tile_fn(tm), seqlen, tm, 512)
    if ctx is None:
        return x, (k_new, v_new, s_fin)
    return x, None


def kernel(x_prompt, x_sample, cache_k, cache_v, state_delta, c, c_ctx, w_mod, b_mod, norm_gains, w_in, lambda_qk,
           g_diff, conv_qkv, a_log, dt_bias, g_delta, w_pool, pool_scale, w_out, w_up, conv_ffn, w_down):
    batch, seq, _ = x_prompt.shape
    dec_batch, dec_seq, _ = x_sample.shape
    p = {'norm_gains': norm_gains, 'lambda_qk': lambda_qk, 'g_diff': g_diff, 'conv_qkv': conv_qkv, 'a_log': a_log,
         'dt_bias': dt_bias, 'g_delta': g_delta, 'w_pool': w_pool, 'pool_scale': pool_scale, 'conv_ffn': conv_ffn,
         'w_in16': w_in.astype(BF16), 'w_tail': _pack_w_tail(w_in),
         'w_out16': w_out.astype(BF16), 'w_up16': w_up.astype(BF16), 'w_down16': w_down.astype(BF16)}

    assert 1 + dec_batch <= SUBLANES
    cond8 = jnp.concatenate([c_ctx[None, :], c, jnp.zeros((SUBLANES - 1 - dec_batch, D_MODEL), F32)], axis=0)
    mod = _modulation(cond8, w_mod, b_mod).reshape(DEPTH, SUBLANES, 6, 1, D_MODEL)

    ctx_rows = lambda tm: (lambda i: 0)
    lat_rows = lambda tm: (lambda i: 1 + (i * tm) // dec_seq)

    xp = x_prompt.reshape(batch * seq, D_MODEL)
    new_caches = None
    for l in range(DEPTH):
        xp, new_caches = _layer(xp, mod[l], ctx_rows, p, l, batch, seq, None, new_caches)
    xs = x_sample.reshape(dec_batch * dec_seq, D_MODEL)
    for l in range(DEPTH):
        xs, _ = _layer(xs, mod[l], lat_rows, p, l, dec_batch, dec_seq, (cache_k, cache_v, state_delta))
    return (xp.reshape(batch, seq, D_MODEL), xs.reshape(dec_batch, dec_seq, D_MODEL),
            new_caches[0].transpose(0, 1, 3, 2, 4), new_caches[1].transpose(0, 1, 3, 2, 4), new_caches[2])
```

```python
import functools
import math

import jax
import jax.numpy as jnp
from jax import lax
from jax.experimental import pallas as pl
from jax.experimental.pallas import tpu as pltpu

D_MODEL = 2048
DEPTH = 2
GRID_W = 64
H_A = 6
DH_A = 64
DV_A = 2 * DH_A
H_B = 6
DK_B = 128
DV_B = 128
CHUNK = 64
N_POOL = 4
POOL_CH = 128
POOL_WINDOWS = (2, 4, 8, 16)
W_A = H_A * DV_A
W_B = H_B * DV_B
W_C = N_POOL * POOL_CH
D_FF = 5632
ROPE_THETA = 10000.0
EPS = 1e-6
Q_SCALE = DH_A ** -0.5 * math.log2(math.e)

LANES = 128
SUBLANES = 8
SEG = 768
QKVB_W = H_B * (2 * DK_B + DV_B)
Z_W = 8 * SEG
N_MAIN_SEG = 7
ZB_SEG = 6
GATE_COL = N_MAIN_SEG * SEG
POOL_COL = GATE_COL + 2 * LANES
N_MOD = 6 * D_MODEL
VMEM_PHYS = 64 * 1024 * 1024

F32 = jnp.float32
BF16 = jnp.bfloat16


def _cparams(sem, vmem_mb):
    assert vmem_mb * 1024 * 1024 < VMEM_PHYS
    return pltpu.CompilerParams(dimension_semantics=sem, vmem_limit_bytes=vmem_mb * 1024 * 1024)


def _silu(x):
    return x * (1.0 / (1.0 + jnp.exp(-x)))


def _dot(a, b):
    return jnp.dot(a, b, preferred_element_type=F32)


def _dot_nt(a, b):
    return lax.dot_general(a, b, (((1,), (1,)), ((), ())), preferred_element_type=F32)


def _dot_tn(a, b):
    return lax.dot_general(a, b, (((0,), (0,)), ((), ())), preferred_element_type=F32)


def _mod_kernel(c_ref, w_ref, b_ref, o_ref):
    a = _silu(c_ref[...]).astype(BF16)
    o_ref[...] = _dot(a, w_ref[...].astype(BF16)) + b_ref[...]


def _modulation(cond8, w_mod, b_mod):
    tn = 1024
    return pl.pallas_call(
        _mod_kernel,
        grid=(DEPTH, N_MOD // tn),
        in_specs=[
            pl.BlockSpec((SUBLANES, D_MODEL), lambda l, j: (0, 0)),
            pl.BlockSpec((None, D_MODEL, tn), lambda l, j: (l, 0, j)),
            pl.BlockSpec((None, 1, tn), lambda l, j: (l, 0, j)),
        ],
        out_specs=pl.BlockSpec((None, SUBLANES, tn), lambda l, j: (l, 0, j)),
        out_shape=jax.ShapeDtypeStruct((DEPTH, SUBLANES, N_MOD), F32),
        compiler_params=_cparams(("parallel", "parallel"), 40),
        name="modulation",
    )(cond8, w_mod, b_mod.reshape(DEPTH, 1, N_MOD))


NORM_ROWS = 2 * SUBLANES


def _modulated_norm(x, gs, sh):
    ms = jnp.mean(x * x, axis=-1, keepdims=True)
    return x * lax.rsqrt(ms + EPS) * gs + sh


def _row_chunks(n_rows, body):
    for r0 in range(0, n_rows, NORM_ROWS):
        body(r0)


def _norm_residual(y_ref, x_ref, o_ref, gg, n_rows):
    def body(r0):
        rows = pl.ds(r0, NORM_ROWS)
        y = y_ref[rows, :]
        ms = jnp.mean(y * y, axis=-1, keepdims=True)
        o_ref[rows, :] = x_ref[rows, :] + y * lax.rsqrt(ms + EPS) * gg
    _row_chunks(n_rows, body)


def _in_proj_kernel(x_ref, g_ref, sc_ref, sh_ref, w_ref, wt_ref, o_ref, h_ref, *, tm):
    j = pl.program_id(1)

    @pl.when(j == 0)
    def _():
        gs = g_ref[...] * (1.0 + sc_ref[...])
        sh = sh_ref[...]

        def body(r0):
            rows = pl.ds(r0, NORM_ROWS)
            h_ref[rows, :] = _modulated_norm(x_ref[rows, :], gs, sh).astype(BF16)
        _row_chunks(tm, body)

    @pl.when(j < N_MAIN_SEG)
    def _():
        o_ref[...] = _dot(h_ref[...], w_ref[...])

    @pl.when(j == N_MAIN_SEG)
    def _():
        o_ref[...] = _dot(h_ref[...], wt_ref[...])


def _mod_spec(which, row_of_tile):
    return pl.BlockSpec((None, None, 1, D_MODEL), lambda i, *_: (row_of_tile(i), which, 0, 0))


def _gain_spec(k):
    return pl.BlockSpec((None, 1, D_MODEL), lambda i, *_: (k, 0, 0))


def _in_proj(x, gains, mod, w_in16, w_tail, layer, row_of_tile, tm):
    t = x.shape[0]
    return pl.pallas_call(
        functools.partial(_in_proj_kernel, tm=tm),
        grid=(t // tm, Z_W // SEG),
        in_specs=[
            pl.BlockSpec((tm, D_MODEL), lambda i, j: (i, 0)),
            _gain_spec(0),
            _mod_spec(1, row_of_tile),
            _mod_spec(0, row_of_tile),
            pl.BlockSpec((None, D_MODEL, SEG), lambda i, j: (layer, 0, jnp.minimum(j, N_MAIN_SEG - 1))),
            pl.BlockSpec((None, D_MODEL, SEG), lambda i, j: (layer, 0, 0)),
        ],
        out_specs=pl.BlockSpec((tm, SEG), lambda i, j: (i, j)),
        out_shape=jax.ShapeDtypeStruct((t, Z_W), F32),
        scratch_shapes=[pltpu.VMEM((tm, D_MODEL), BF16)],
        compiler_params=_cparams(("parallel", "arbitrary"), 48),
        name="in_proj",
    )(x, gains, mod, mod, w_in16, w_tail)


def _rope_tables(n):
    rows = n // GRID_W
    row = jnp.repeat(jnp.arange(rows), GRID_W)
    col = jnp.arange(rows * GRID_W) % GRID_W
    n_pair = DH_A // 4
    inv = ROPE_THETA ** (-jnp.arange(n_pair, dtype=F32) / n_pair)
    ang = jnp.concatenate([row[:, None] * inv, col[:, None] * inv], axis=-1)
    cos = jnp.repeat(jnp.cos(ang), 2, axis=-1)
    sin = jnp.repeat(jnp.sin(ang), 2, axis=-1) * jnp.tile(jnp.array([-1.0, 1.0], F32), DH_A // 2)
    return jnp.tile(cos, (1, 2)), jnp.tile(sin, (1, 2))


def _rope_kernel(q_ref, k_ref, v_ref, cos_ref, sin_ref, qo_ref, ko_ref, vo_ref):
    cos = cos_ref[...]
    sin = sin_ref[...]
    even = (lax.broadcasted_iota(jnp.int32, cos.shape, 1) % 2) == 0
    for h in range(H_A):
        sl = slice(h * LANES, (h + 1) * LANES)
        for src, dst, scale in ((q_ref, qo_ref, Q_SCALE), (k_ref, ko_ref, None)):
            x = src[:, sl]
            swapped = jnp.where(even, pltpu.roll(x, LANES - 1, 1), pltpu.roll(x, 1, 1))
            y = x * cos + swapped * sin
            if scale is not None:
                y = y * scale
            dst[:, sl] = y.astype(BF16)
        vo_ref[sl, :] = v_ref[:, sl].T.astype(BF16)


def _rope_prep(z, seqlen):
    t = z.shape[0]
    tm = 512
    cos, sin = _rope_tables(seqlen)
    per_seq = seqlen // tm
    tab_spec = pl.BlockSpec((tm, LANES), lambda i: (i % per_seq, 0))
    out = jax.ShapeDtypeStruct((t, SEG), BF16)
    return pl.pallas_call(
        _rope_kernel,
        grid=(t // tm,),
        in_specs=[pl.BlockSpec((tm, SEG), lambda i: (i, 0)),
                  pl.BlockSpec((tm, SEG), lambda i: (i, 1)),
                  pl.BlockSpec((tm, SEG), lambda i: (i, 2)),
                  tab_spec, tab_spec],
        out_specs=[pl.BlockSpec((tm, SEG), lambda i: (i, 0)), pl.BlockSpec((tm, SEG), lambda i: (i, 0)),
                   pl.BlockSpec((SEG, tm), lambda i: (0, i))],
        out_shape=[out, out, jax.ShapeDtypeStruct((SEG, t), BF16)],
        compiler_params=_cparams(("parallel",), 32),
        name="rope_prep",
    )(z, z, z, cos, sin)


def _attn_kernel(*refs, n_pieces, n_alias, cache_slot, q_prepped, v_transposed, head_major, lam_init, sub, hps):
    lq_ref, gd_ref, q_ref = refs[:3]
    kv_refs = refs[3:3 + 2 * n_pieces]
    o_ref = refs[3 + 2 * n_pieces + n_alias]
    lq = lq_ref[...]
    lam = (jnp.exp(jnp.sum(lq[0:1] * lq[1:2], axis=-1, keepdims=True))
           - jnp.exp(jnp.sum(lq[2:3] * lq[3:4], axis=-1, keepdims=True)) + lam_init)
    lo = lax.broadcasted_iota(jnp.int32, (sub, LANES), 1) < DH_A
    ones_rows = 2 * SUBLANES

    def head_cols(hh):
        return slice(hh * LANES, (hh + 1) * LANES)

    def keys(hh, p):
        k_ref = kv_refs[2 * p]
        return (k_ref[hh] if head_major[p] else k_ref[:, head_cols(hh)]).astype(BF16)

    def v_aug(hh, p):
        v_ref = kv_refs[2 * p + 1]
        if v_transposed[p]:
            vt = v_ref[head_cols(hh), :]
        else:
            vt = (v_ref[hh] if head_major[p] else v_ref[:, head_cols(hh)]).astype(F32).T
        vt = vt.astype(BF16)
        return jnp.concatenate([vt, jnp.ones((ones_rows, vt.shape[1]), BF16)], axis=0)

    def qk(u):
        hh, r = u
        q = q_ref[r * sub:(r + 1) * sub, head_cols(hh)]
        if not q_prepped:
            q = q * Q_SCALE
        zero = jnp.zeros_like(q)
        qq = jnp.concatenate([jnp.where(lo, q, zero), jnp.where(lo, zero, q)], axis=0).astype(BF16)
        return [_dot_nt(keys(hh, p), qq) for p in range(n_pieces)]

    def finish(u, scores, vts):
        hh, r = u
        m = scores[0].max(axis=0, keepdims=True)
        for s in scores[1:]:
            m = jnp.maximum(m, s.max(axis=0, keepdims=True))
        acc = None
        for p in range(n_pieces):
            e = jnp.exp2((scores[p] - m).astype(BF16))
            t = _dot(vts[p], e)
            acc = t if acc is None else acc + t
        l = acc[DV_A:DV_A + 1, :]
        ot = acc[:DV_A, :sub] * (1.0 / l[:, :sub]) - acc[:DV_A, sub:] * (lam / l[:, sub:])
        o = ot.T
        ms = jnp.mean(o * o, axis=-1, keepdims=True)
        o = o * lax.rsqrt(ms + EPS) * gd_ref[...] * (1.0 - lam_init)
        o_ref[r * sub:(r + 1) * sub, head_cols(hh)] = o.astype(o_ref.dtype)

    if cache_slot is not None:
        for c_ref, src in zip(refs[4 + 2 * n_pieces + n_alias:], kv_refs[:2]):
            dst = c_ref if n_alias else c_ref.at[cache_slot]
            for hh in range(hps):
                dst[hh] = src[:, head_cols(hh)]
            if not n_alias:
                for other in range(c_ref.shape[0]):
                    if other != cache_slot:
                        c_ref[other] = jnp.zeros(c_ref.shape[1:], c_ref.dtype)

    nsub = q_ref.shape[0] // sub
    units = [(hh, r) for hh in range(hps) for r in range(nsub)]
    nxt = qk(units[0])
    vts = None
    for n, u in enumerate(units):
        cur = nxt
        if u[1] == 0:
            vts = [v_aug(u[0], p) for p in range(n_pieces)]
        if n + 1 < len(units):
            nxt = qk(units[n + 1])
        finish(u, cur, vts)


def _attention(q_arr, pieces, v_transposed, head_major, lambda_qk_l, g_diff_l, n_seq, seqlen, tq, hps, q_prepped,
               lam_init, cache_out=None):
    per_seq = seqlen // tq
    width = hps * LANES
    in_specs = [
        pl.BlockSpec((4, DH_A), lambda b, h, i: (0, 0)),
        pl.BlockSpec((1, DV_A), lambda b, h, i: (0, 0)),
        pl.BlockSpec((tq, width), lambda b, h, i: (b * per_seq + i, h)),
    ]
    args = [lambda_qk_l, g_diff_l.reshape(1, DV_A), q_arr]
    for k_arr, v_arr, k_spec, v_spec in pieces:
        in_specs += [k_spec, v_spec]
        args += [k_arr, v_arr]
    out_specs = [pl.BlockSpec((tq, width), lambda b, h, i: (b * per_seq + i, h))]
    out_shape = [jax.ShapeDtypeStruct((n_seq * seqlen, W_A), BF16)]
    aliases = {}
    n_alias = 0
    if cache_out is not None:
        layer, caches = cache_out
        assert hps == H_A and per_seq == 1
        out_shape += [jax.ShapeDtypeStruct((n_seq, DEPTH, H_A, seqlen, LANES), F32)] * 2
        if caches is None:
            out_specs += [pl.BlockSpec((None, DEPTH, H_A, seqlen, LANES), lambda b, h, i: (b, 0, 0, 0, 0))] * 2
        else:
            out_specs += [pl.BlockSpec((None, None, H_A, seqlen, LANES), lambda b, h, i: (b, layer, 0, 0, 0))] * 2
            n_alias = 2
            aliases = {len(args): 1, len(args) + 1: 2}
            in_specs += [pl.BlockSpec(memory_space=pl.ANY)] * 2
            args += list(caches)
    return pl.pallas_call(
        functools.partial(_attn_kernel, n_pieces=len(pieces), n_alias=n_alias,
                          cache_slot=None if cache_out is None else cache_out[0],
                          q_prepped=q_prepped, v_transposed=v_transposed, head_major=head_major, lam_init=lam_init,
                          sub=128, hps=hps),
        grid=(n_seq, H_A // hps, per_seq),
        in_specs=in_specs,
        out_specs=out_specs,
        out_shape=out_shape,
        input_output_aliases=aliases,
        compiler_params=_cparams(("parallel", "parallel", "arbitrary"), 48),
        name="diff_attention",
    )(*args)


def _pool_kernel(u_ref, w_ref, sc_ref, o_ref, *, seqlen):
    rows = u_ref.shape[0]
    pos = lax.broadcasted_iota(jnp.int32, (rows, POOL_CH), 0) % seqlen
    for g, win in enumerate(POOL_WINDOWS):
        sl = slice(g * POOL_CH, (g + 1) * POOL_CH)
        u = u_ref[:, sl]
        half = win // 2
        acc = u
        for d in range(-half, half):
            if d == 0:
                continue
            shifted = pltpu.roll(u, (-d) % rows, 0)
            valid = (pos + d >= 0) & (pos + d < seqlen)
            acc = acc + jnp.where(valid, shifted, 0.0)
        cnt = (jnp.minimum(pos + half, seqlen) - jnp.maximum(pos - half, 0)).astype(F32)
        y = acc / cnt - u
        o_ref[:, sl] = (_dot(y.astype(BF16), w_ref[g].astype(BF16)) * sc_ref[:, sl]).astype(BF16)


def _pool(z, w_pool_l, pool_scale_l, seqlen, rows):
    t = z.shape[0]
    col_block = POOL_COL // W_C
    assert col_block * W_C == POOL_COL
    return pl.pallas_call(
        functools.partial(_pool_kernel, seqlen=seqlen),
        grid=(t // rows,),
        in_specs=[pl.BlockSpec((rows, W_C), lambda i: (i, col_block)),
                  pl.BlockSpec((N_POOL, POOL_CH, POOL_CH), lambda i: (0, 0, 0)),
                  pl.BlockSpec((1, W_C), lambda i: (0, 0))],
        out_specs=pl.BlockSpec((rows, W_C), lambda i: (i, 0)),
        out_shape=jax.ShapeDtypeStruct((t, W_C), BF16),
        compiler_params=_cparams(("parallel",), 48),
        name="pool_mixer",
    )(z, w_pool_l, pool_scale_l.reshape(1, W_C))


def _shift_rows(x, prev_row, next_row, pos, seqlen):
    tm = x.shape[0]
    ridx = lax.broadcasted_iota(jnp.int32, x.shape, 0)
    xm = jnp.where(ridx == 0, prev_row, pltpu.roll(x, 1, 0))
    xp = jnp.where(ridx == tm - 1, next_row, pltpu.roll(x, tm - 1, 0))
    xm = jnp.where(pos == 0, 0.0, xm)
    xp = jnp.where(pos == seqlen - 1, 0.0, xp)
    return xm, xp


def _halo_specs(tm, width, col_block, t):
    nb = t // SUBLANES
    per = tm // SUBLANES
    prev = pl.BlockSpec((SUBLANES, width), lambda i, *_: (jnp.maximum(i * per - 1, 0), col_block))
    nxt = pl.BlockSpec((SUBLANES, width), lambda i, *_: (jnp.minimum((i + 1) * per, nb - 1), col_block))
    return prev, nxt


def _dn_prep_kernel(x_ref, xprev_ref, xnext_ref, gate_ref, cw_ref, gp_ref, qkv_ref, gb_ref, *, seqlen, tm):
    i = pl.program_id(0)
    cw = cw_ref[...]
    cblk = 256
    for cb in range(QKVB_W // cblk):
        sl = slice(cb * cblk, (cb + 1) * cblk)
        x = x_ref[:, sl]
        pos = (i * tm + lax.broadcasted_iota(jnp.int32, x.shape, 0)) % seqlen
        xm, xp = _shift_rows(x, xprev_ref[SUBLANES - 1:SUBLANES, sl], xnext_ref[0:1, sl], pos, seqlen)
        y = _silu(xm * cw[0:1, sl] + x * cw[1:2, sl] + xp * cw[2:3, sl])
        for hh in range(cblk // LANES):
            c0 = cb * cblk + hh * LANES
            yh = y[:, hh * LANES:(hh + 1) * LANES]
            if c0 < 2 * H_B * DK_B:
                yh = yh * lax.rsqrt(jnp.sum(yh * yh, axis=-1, keepdims=True) + EPS)
                if c0 < H_B * DK_B:
                    yh = yh * (DK_B ** -0.5)
            qkv_ref[:, c0:c0 + LANES] = yh
    lane = lax.broadcasted_iota(jnp.int32, (tm, LANES), 1)
    for d in range(2):
        raw = gate_ref[:, d * LANES:(d + 1) * LANES]
        gp = gp_ref[d]
        beta = 1.0 / (1.0 + jnp.exp(-raw))
        xa = raw + gp[1:2]
        softplus = jnp.maximum(xa, 0.0) + jnp.log1p(jnp.exp(-jnp.abs(xa)))
        gdec = -jnp.exp(gp[0:1]) * softplus
        gb_ref[d] = jnp.where(lane < H_B, beta, jnp.where(lane < 2 * H_B, gdec, 0.0))


def _dn_prep(z, conv_qkv_l, gate_params, seqlen, tm):
    t = z.shape[0]
    prev, nxt = _halo_specs(tm, QKVB_W, 1, t)
    return pl.pallas_call(
        functools.partial(_dn_prep_kernel, seqlen=seqlen, tm=tm),
        grid=(t // tm,),
        in_specs=[pl.BlockSpec((tm, QKVB_W), lambda i: (i, 1)), prev, nxt,
                  pl.BlockSpec((tm, 2 * LANES), lambda i: (i, GATE_COL // (2 * LANES))),
                  pl.BlockSpec((3, QKVB_W), lambda i: (0, 0)),
                  pl.BlockSpec((2, 2, LANES), lambda i: (0, 0, 0))],
        out_specs=[pl.BlockSpec((tm, QKVB_W), lambda i: (i, 0)),
                   pl.BlockSpec((2, tm, LANES), lambda i: (0, i, 0))],
        out_shape=[jax.ShapeDtypeStruct((t, QKVB_W), F32), jax.ShapeDtypeStruct((2, t, LANES), F32)],
        compiler_params=_cparams(("parallel",), 48),
        name="deltanet_prep",
    )(z, z, z, z, conv_qkv_l, gate_params)


def _dn_scan_kernel(*refs, cs, has_s0, state_slot):
    qkv_refs, gb_refs = refs[0:2], refs[2:4]
    s0_ref = refs[4] if has_s0 else None
    o_refs, sfin_ref, s_scr = refs[-4:-2], refs[-2], refs[-1]
    st = pl.program_id(1)

    @pl.when(st == 0)
    def _():
        if has_s0:
            s_scr[...] = s0_ref[...]
        else:
            s_scr[...] = jnp.zeros_like(s_scr)

    ii = lax.broadcasted_iota(jnp.int32, (CHUNK, CHUNK), 0)
    jj = lax.broadcasted_iota(jnp.int32, (CHUNK, CHUNK), 1)
    incl_d = (ii >= jj, ii <= jj)
    strict_d = (ii > jj, ii < jj)
    eye = (ii == jj).astype(F32)
    eq_masks = [((ii >> (3 + k)) == (jj >> (3 + k))).astype(F32) for k in range(3)] + [jnp.ones((CHUNK, CHUNK), F32)]
    offs = []
    st_q, st_k, st_v, st_beta, st_gc, st_gr, st_gt, st_d = [], [], [], [], [], [], [], []
    for d in range(2):
        tri = incl_d[d].astype(F32)
        for pos in range(cs):
            off = (pos if d == 0 else cs - 1 - pos) * CHUNK
            offs.append(off)
            gbc = gb_refs[d][off:off + CHUNK, :]
            gcol = jnp.dot(tri, gbc, preferred_element_type=F32, precision=lax.Precision.HIGHEST)
            grow = gcol.T
            gtot = jnp.sum(gbc, axis=0, keepdims=True)
            for h in range(H_B):
                st_d.append(d)
                st_q.append(qkv_refs[d][off:off + CHUNK, h * DK_B:(h + 1) * DK_B])
                st_k.append(qkv_refs[d][off:off + CHUNK, (H_B + h) * DK_B:(H_B + h + 1) * DK_B])
                st_v.append(qkv_refs[d][off:off + CHUNK, (2 * H_B + h) * DK_B:(2 * H_B + h + 1) * DK_B])
                st_beta.append(gbc[:, h:h + 1])
                st_gc.append(gcol[:, H_B + h:H_B + h + 1])
                st_gr.append(grow[H_B + h:H_B + h + 1, :])
                st_gt.append(gtot[:, H_B + h:H_B + h + 1])
    rng = range(2 * cs * H_B)
    st_eg = [jnp.exp(st_gc[i]) for i in rng]
    st_kq = [_dot_nt(jnp.concatenate([st_k[i], st_q[i]], axis=0).astype(BF16), st_k[i].astype(BF16)) for i in rng]
    st_lmat, st_a, st_x0 = [], [], []
    for i in rng:
        incl, strict = incl_d[st_d[i]], strict_d[st_d[i]]
        dec = jnp.where(incl, jnp.exp(jnp.where(incl, st_gc[i] - st_gr[i], 0.0)), 0.0)
        st_lmat.append(jnp.where(strict, st_kq[i][:CHUNK] * dec, 0.0) * st_beta[i])
        st_a.append((st_kq[i][CHUNK:] * dec).astype(BF16))
        st_x0.append(jnp.concatenate([st_v[i] * st_beta[i], st_k[i] * (st_beta[i] * st_eg[i])], axis=1))
    st_lbf = [st_lmat[i] * eq_masks[0] for i in rng]
    st_lb = [st_lbf[i].astype(BF16) for i in rng]
    st_p2f = [_dot(st_lb[i], st_lb[i]) for i in rng]
    st_p2 = [st_p2f[i].astype(BF16) for i in rng]
    st_p3 = [_dot(st_lb[i], st_p2[i]) for i in rng]
    st_p4 = [_dot(st_p2[i], st_p2[i]).astype(BF16) for i in rng]
    st_t = [eye - st_lbf[i] + st_p2f[i] - st_p3[i] for i in rng]
    st_t = [st_t[i] + _dot(st_t[i].astype(BF16), st_p4[i]) for i in rng]
    for lvl in range(3):
        cmask = eq_masks[lvl + 1] - eq_masks[lvl]
        st_t16 = [st_t[i].astype(BF16) for i in rng]
        st_m = [_dot((st_lmat[i] * cmask).astype(BF16), st_t16[i]).astype(BF16) for i in rng]
        st_t = [st_t[i] - _dot(st_t16[i], st_m[i]) for i in rng]
    st_x = [st_x0[i] + _dot((st_t[i] - eye).astype(BF16), st_x0[i].astype(BF16)) for i in rng]
    s_cur = [s_scr[d, h] for d in range(2) for h in range(H_B)]
    nch = range(2 * H_B)
    for pos in range(cs):
        idx = [(d * cs + pos) * H_B + h for d in range(2) for h in range(H_B)]
        wq = [jnp.concatenate([st_x[i][:, DV_B:], st_q[i] * st_eg[i]], axis=0).astype(BF16) for i in idx]
        r = [_dot(wq[c], s_cur[c].astype(BF16)) for c in nch]
        vn16 = [(st_x[i][:, :DV_B] - r[c][:CHUNK]).astype(BF16) for c, i in enumerate(idx)]
        o = [r[c][CHUNK:] + _dot(st_a[i], vn16[c]) for c, i in enumerate(idx)]
        kd16 = [(st_k[i] * jnp.exp(st_gt[i] - st_gc[i])).astype(BF16) for i in idx]
        s_cur = [s_cur[c] * jnp.exp(st_gt[i]) + _dot_tn(kd16[c], vn16[c]) for c, i in enumerate(idx)]
        for c in nch:
            d, h = divmod(c, H_B)
            off = offs[d * cs + pos]
            o_refs[d][off:off + CHUNK, h * DV_B:(h + 1) * DV_B] = o[c].astype(BF16)
    for c in nch:
        s_scr[c // H_B, c % H_B] = s_cur[c]

    @pl.when(st == pl.num_programs(1) - 1)
    def _():
        if state_slot is None:
            sfin_ref[...] = s_scr[...]
        else:
            for other in range(sfin_ref.shape[0]):
                sfin_ref[other] = s_scr[...] if other == state_slot else jnp.zeros_like(s_scr)


def _dn_scan(qkv, gb, s0, layer, n_seq, seqlen, cs, state_out=None):
    t = qkv.shape[0]
    rows = cs * CHUNK
    nsteps = seqlen // rows
    fwd = lambda b, s: b * nsteps + s
    bwd = lambda b, s: b * nsteps + nsteps - 1 - s
    in_specs = [pl.BlockSpec((rows, QKVB_W), lambda b, s: (fwd(b, s), 0)),
                pl.BlockSpec((rows, QKVB_W), lambda b, s: (bwd(b, s), 0)),
                pl.BlockSpec((None, rows, LANES), lambda b, s: (0, fwd(b, s), 0)),
                pl.BlockSpec((None, rows, LANES), lambda b, s: (1, bwd(b, s), 0))]
    args = [qkv, qkv, gb, gb]
    if s0 is not None:
        in_specs.append(pl.BlockSpec((None, None, 2, H_B, DK_B, DV_B), lambda b, s: (b, layer, 0, 0, 0, 0)))
        args.append(s0)
    o_shape = jax.ShapeDtypeStruct((t, W_B), BF16)
    aliases = {}
    state_slot = None
    if state_out is None:
        s_spec = pl.BlockSpec((None, 2, H_B, DK_B, DV_B), lambda b, s: (b, 0, 0, 0, 0))
        s_shape = jax.ShapeDtypeStruct((n_seq, 2, H_B, DK_B, DV_B), F32)
    else:
        s_layer, s_prev = state_out
        s_shape = jax.ShapeDtypeStruct((n_seq, DEPTH, 2, H_B, DK_B, DV_B), F32)
        if s_prev is None:
            state_slot = s_layer
            s_spec = pl.BlockSpec((None, DEPTH, 2, H_B, DK_B, DV_B), lambda b, s: (b, 0, 0, 0, 0, 0))
        else:
            s_spec = pl.BlockSpec((None, None, 2, H_B, DK_B, DV_B), lambda b, s: (b, s_layer, 0, 0, 0, 0))
            aliases = {len(args): 2}
            in_specs.append(pl.BlockSpec(memory_space=pl.ANY))
            args.append(s_prev)
    return pl.pallas_call(
        functools.partial(_dn_scan_kernel, cs=cs, has_s0=s0 is not None, state_slot=state_slot),
        grid=(n_seq, nsteps),
        in_specs=in_specs,
        out_specs=[pl.BlockSpec((rows, W_B), lambda b, s: (fwd(b, s), 0)),
                   pl.BlockSpec((rows, W_B), lambda b, s: (bwd(b, s), 0)),
                   s_spec],
        out_shape=[o_shape, o_shape, s_shape],
        input_output_aliases=aliases,
        scratch_shapes=[pltpu.VMEM((2, H_B, DK_B, DV_B), F32)],
        compiler_params=_cparams(("parallel", "arbitrary"), 40),
        name="deltanet_scan",
    )(*args)


def _out_proj_kernel(oa_ref, of_ref, ob_ref, z_ref, oc_ref, x_ref, gt_ref, g1_ref, gd_ref, w_ref, o_ref, mix_ref):
    mix_ref[:, 0:W_A] = oa_ref[...]
    gd = gd_ref[...]
    for h in range(H_B):
        sl = slice(h * DV_B, (h + 1) * DV_B)
        ob = of_ref[:, sl].astype(F32) + ob_ref[:, sl].astype(F32)
        ms = jnp.mean(ob * ob, axis=-1, keepdims=True)
        ob = ob * lax.rsqrt(ms + EPS) * gd * _silu(z_ref[:, sl])
        mix_ref[:, W_A + h * DV_B:W_A + (h + 1) * DV_B] = ob.astype(BF16)
    mix_ref[:, W_A + W_B:] = oc_ref[...]
    o_ref[...] = _dot(mix_ref[...], w_ref[...])
    _norm_residual(o_ref, x_ref, o_ref, gt_ref[...] * g1_ref[...], o_ref.shape[0])


def _out_proj(oa, o_f, o_b, z, oc, x, gains, mod, g_delta_l, w_out16, layer, row_of_tile, tm):
    t = x.shape[0]
    return pl.pallas_call(
        _out_proj_kernel,
        grid=(t // tm,),
        in_specs=[pl.BlockSpec((tm, W_A), lambda i: (i, 0)),
                  pl.BlockSpec((tm, W_B), lambda i: (i, 0)),
                  pl.BlockSpec((tm, W_B), lambda i: (i, 0)),
                  pl.BlockSpec((tm, SEG), lambda i: (i, ZB_SEG)),
                  pl.BlockSpec((tm, W_C), lambda i: (i, 0)),
                  pl.BlockSpec((tm, D_MODEL), lambda i: (i, 0)),
                  _mod_spec(2, row_of_tile),
                  _gain_spec(1),
                  pl.BlockSpec((1, DV_B), lambda i: (0, 0)),
                  pl.BlockSpec((None, D_MODEL, D_MODEL), lambda i: (layer, 0, 0))],
        out_specs=pl.BlockSpec((tm, D_MODEL), lambda i: (i, 0)),
        out_shape=jax.ShapeDtypeStruct((t, D_MODEL), F32),
        scratch_shapes=[pltpu.VMEM((tm, D_MODEL), BF16)],
        compiler_params=_cparams(("parallel",), 56),
        name="out_proj",
    )(oa, o_f, o_b, z, oc, x, mod, gains, g_delta_l.reshape(1, DV_B), w_out16)


def _ffn_kernel(x_ref, xprev_ref, xnext_ref, g2_ref, sc_ref, sh_ref, gt_ref, g3_ref,
                wg_ref, wu_ref, cg_ref, cu_ref, wd_ref, o_ref, h_ref, p_ref, acc_ref, *, seqlen, tm):
    i = pl.program_id(0)
    j = pl.program_id(1)
    hb = SUBLANES

    @pl.when(j == 0)
    def _():
        gs = g2_ref[...] * (1.0 + sc_ref[...])
        sh = sh_ref[...]
        norm = lambda v: _modulated_norm(v, gs, sh).astype(BF16)
        h_ref[0:hb, :] = norm(xprev_ref[...])
        for r0 in range(0, tm, NORM_ROWS):
            h_ref[hb + r0:hb + r0 + NORM_ROWS, :] = norm(x_ref[r0:r0 + NORM_ROWS, :])
        h_ref[hb + tm:, :] = norm(xnext_ref[...])
        acc_ref[...] = jnp.zeros_like(acc_ref)

    tf = wg_ref.shape[1]
    pos = (i * tm + lax.broadcasted_iota(jnp.int32, (tm, tf), 0)) % seqlen
    has_prev = pos != 0
    has_next = pos != seqlen - 1

    def conv_branch(w_ref, c_ref):
        p_ref[...] = _dot(h_ref[...], w_ref[...])
        cw = c_ref[...]
        return (jnp.where(has_prev, p_ref[hb - 1:hb - 1 + tm, :], 0.0) * cw[0:1]
                + p_ref[hb:hb + tm, :] * cw[1:2]
                + jnp.where(has_next, p_ref[hb + 1:hb + 1 + tm, :], 0.0) * cw[2:3])

    gate = conv_branch(wg_ref, cg_ref)
    up = conv_branch(wu_ref, cu_ref)
    acc_ref[...] += _dot((_silu(gate) * up).astype(BF16), wd_ref[...])

    @pl.when(j == pl.num_programs(1) - 1)
    def _():
        _norm_residual(acc_ref, x_ref, o_ref, gt_ref[...] * g3_ref[...], tm)


def _ffn(x, gains, mod, w_up16, conv_ffn, w_down16, layer, row_of_tile, seqlen, tm, tf):
    t = x.shape[0]
    nf = D_FF // tf
    prev, nxt = _halo_specs(tm, D_MODEL, 0, t)
    return pl.pallas_call(
        functools.partial(_ffn_kernel, seqlen=seqlen, tm=tm),
        grid=(t // tm, nf),
        in_specs=[pl.BlockSpec((tm, D_MODEL), lambda i, j: (i, 0)), prev, nxt,
                  _gain_spec(2), _mod_spec(4, row_of_tile), _mod_spec(3, row_of_tile), _mod_spec(5, row_of_tile),
                  _gain_spec(3),
                  pl.BlockSpec((None, D_MODEL, tf), lambda i, j: (layer, 0, j)),
                  pl.BlockSpec((None, D_MODEL, tf), lambda i, j: (layer, 0, nf + j)),
                  pl.BlockSpec((None, 3, tf), lambda i, j: (layer, 0, j)),
                  pl.BlockSpec((None, 3, tf), lambda i, j: (layer, 0, nf + j)),
                  pl.BlockSpec((None, tf, D_MODEL), lambda i, j: (layer, j, 0))],
        out_specs=pl.BlockSpec((tm, D_MODEL), lambda i, j: (i, 0)),
        out_shape=jax.ShapeDtypeStruct((t, D_MODEL), F32),
        scratch_shapes=[pltpu.VMEM((tm + 2 * SUBLANES, D_MODEL), BF16),
                        pltpu.VMEM((tm + 2 * SUBLANES, tf), F32),
                        pltpu.VMEM((tm, D_MODEL), F32)],
        compiler_params=_cparams(("parallel", "arbitrary"), 56),
        name="conv_ffn",
    )(x, x, x, gains, mod, mod, mod, gains, w_up16, w_up16, conv_ffn, conv_ffn, w_down16)


def _pack_w_tail(w_in):
    tail = w_in[:, :, N_MAIN_SEG * SEG:]
    beta = tail[:, :, :2 * H_B]
    alpha = tail[:, :, 2 * H_B:4 * H_B]
    pool = tail[:, :, 4 * H_B:]
    pad = jnp.zeros((DEPTH, D_MODEL, LANES - 2 * H_B), w_in.dtype)
    cols = []
    for d in range(2):
        cols += [beta[:, :, d * H_B:(d + 1) * H_B], alpha[:, :, d * H_B:(d + 1) * H_B], pad]
    cols.append(pool)
    return jnp.concatenate(cols, axis=2).astype(BF16)


def _gate_params(a_log_l, dt_bias_l):
    gp = jnp.zeros((2, 2, LANES), F32)
    gp = gp.at[:, 0, H_B:2 * H_B].set(a_log_l.astype(F32))
    return gp.at[:, 1, H_B:2 * H_B].set(dt_bias_l.astype(F32))


def _layer(x, mod_l, row_of_tile_fn, p, l, n_seq, seqlen, ctx, new_caches=None):
    t = x.shape[0]
    gains = p['norm_gains'][l].reshape(4, 1, D_MODEL)
    lam_init = 0.8 - 0.6 * math.exp(-0.3 * l)
    tm_big = 1024
    assert t % tm_big == 0 and (seqlen % tm_big == 0 or tm_big % seqlen == 0)
    z = _in_proj(x, gains, mod_l, p['w_in16'], p['w_tail'], l, row_of_tile_fn(tm_big), tm_big)

    if ctx is None:
        kv_spec = lambda seg: pl.BlockSpec((seqlen, W_A), lambda b, h, i: (b, seg))
        pieces = [(z, z, kv_spec(1), kv_spec(2))]
        kv_prev = None if new_caches is None else new_caches[:2]
        oa, k_new, v_new = _attention(z, pieces, (False,), (False,), p['lambda_qk'][l], p['g_diff'][l], n_seq, seqlen,
                                      seqlen, H_A, False, lam_init, cache_out=(l, kv_prev))
    else:
        cache_k, cache_v, state_delta = ctx
        past = cache_k.shape[2]
        q_r, k_r, v_t = _rope_prep(z, seqlen)
        hps = 2
        lat_spec = pl.BlockSpec((seqlen, hps * LANES), lambda b, h, i: (b, h))
        vt_spec = pl.BlockSpec((hps * LANES, seqlen), lambda b, h, i: (h, b))
        ck = cache_k.transpose(0, 1, 3, 2, 4)
        cv = cache_v.transpose(0, 1, 3, 2, 4)
        c_spec = pl.BlockSpec((None, None, hps, past, LANES), lambda b, h, i: (b, l, h, 0, 0))
        pieces = [(k_r, v_t, lat_spec, vt_spec), (ck, cv, c_spec, c_spec)]
        oa, = _attention(q_r, pieces, (True, False), (False, True), p['lambda_qk'][l], p['g_diff'][l], n_seq, seqlen,
                         1024, hps, True, lam_init)

    qkv, gb = _dn_prep(z, p['conv_qkv'][l], _gate_params(p['a_log'][l], p['dt_bias'][l]), seqlen, tm_big)
    s_prev = None if new_caches is None else new_caches[2]
    o_f, o_b, s_fin = _dn_scan(qkv, gb, None if ctx is None else ctx[2], l, n_seq, seqlen, 4,
                               state_out=(l, s_prev) if ctx is None else None)

    oc = _pool(z, p['w_pool'][l], p['pool_scale'][l], seqlen, max(seqlen, 1024))

    tm = 256
    x = _out_proj(oa, o_f, o_b, z, oc, x, gains, mod_l, p['g_delta'][l], p['w_out16'], l, row_of_tile_fn(tm), tm)
    tm = 512
    x = _ffn(x, gains, mod_l, p['w_up16'], p['conv_ffn'], p['w_down16'], l, row_of_tile_fn(tm), seqlen, tm, 512)
    if ctx is None:
        return x, (k_new, v_new, s_fin)
    return x, None


def kernel(x_prompt, x_sample, cache_k, cache_v, state_delta, c, c_ctx, w_mod, b_mod, norm_gains, w_in, lambda_qk,
           g_diff, conv_qkv, a_log, dt_bias, g_delta, w_pool, pool_scale, w_out, w_up, conv_ffn, w_down):
    batch, seq, _ = x_prompt.shape
    dec_batch, dec_seq, _ = x_sample.shape
    p = {'norm_gains': norm_gains, 'lambda_qk': lambda_qk, 'g_diff': g_diff, 'conv_qkv': conv_qkv, 'a_log': a_log,
         'dt_bias': dt_bias, 'g_delta': g_delta, 'w_pool': w_pool, 'pool_scale': pool_scale, 'conv_ffn': conv_ffn,
         'w_in16': w_in.astype(BF16), 'w_tail': _pack_w_tail(w_in),
         'w_out16': w_out.astype(BF16), 'w_up16': w_up.astype(BF16), 'w_down16': w_down.astype(BF16)}

    assert 1 + dec_batch <= SUBLANES
    cond8 = jnp.concatenate([c_ctx[None, :], c, jnp.zeros((SUBLANES - 1 - dec_batch, D_MODEL), F32)], axis=0)
    mod = _modulation(cond8, w_mod, b_mod).reshape(DEPTH, SUBLANES, 6, 1, D_MODEL)

    ctx_rows = lambda tm: (lambda i: 0)
    lat_rows = lambda tm: (lambda i: 1 + (i * tm) // dec_seq)

    xp = x_prompt.reshape(batch * seq, D_MODEL)
    new_caches = None
    for l in range(DEPTH):
        xp, new_caches = _layer(xp, mod[l], ctx_rows, p, l, batch, seq, None, new_caches)
    xs = x_sample.reshape(dec_batch * dec_seq, D_MODEL)
    for l in range(DEPTH):
        xs, _ = _layer(xs, mod[l], lat_rows, p, l, dec_batch, dec_seq, (cache_k, cache_v, state_delta))
    return (xp.reshape(batch, seq, D_MODEL), xs.reshape(dec_batch, dec_seq, D_MODEL),
            new_caches[0].transpose(0, 1, 3, 2, 4), new_caches[1].transpose(0, 1, 3, 2, 4), new_caches[2])
```

```python
import functools
import math

import jax
import jax.numpy as jnp
from jax import lax
from jax.experimental import pallas as pl
from jax.experimental.pallas import tpu as pltpu

D_MODEL = 2048
DEPTH = 2
GRID_W = 64
H_A = 6
DH_A = 64
DV_A = 2 * DH_A
H_B = 6
DK_B = 128
DV_B = 128
CHUNK = 64
N_POOL = 4
POOL_CH = 128
POOL_WINDOWS = (2, 4, 8, 16)
W_A = H_A * DV_A
W_B = H_B * DV_B
W_C = N_POOL * POOL_CH
D_FF = 5632
ROPE_THETA = 10000.0
EPS = 1e-6
LOG2E = math.log2(math.e)
Q_SCALE = DH_A ** -0.5 * LOG2E

LANES = 128
SUBLANES = 8
SEG = 768
QKVB_W = H_B * (2 * DK_B + DV_B)
Z_W = 8 * SEG
N_MAIN_SEG = 7
ZB_SEG = 6
GATE_COL = N_MAIN_SEG * SEG
POOL_COL = GATE_COL + 2 * LANES
N_MOD = 6 * D_MODEL
VMEM_PHYS = 64 * 1024 * 1024

F32 = jnp.float32
BF16 = jnp.bfloat16


def _cparams(sem, vmem_mb):
    assert vmem_mb * 1024 * 1024 < VMEM_PHYS
    return pltpu.CompilerParams(dimension_semantics=sem, vmem_limit_bytes=vmem_mb * 1024 * 1024)


def _silu(x):
    return x * (1.0 / (1.0 + jnp.exp(-x)))


def _dot(a, b):
    return jnp.dot(a, b, preferred_element_type=F32)


def _dot_nt(a, b):
    return lax.dot_general(a, b, (((1,), (1,)), ((), ())), preferred_element_type=F32)


def _dot_tn(a, b):
    return lax.dot_general(a, b, (((0,), (0,)), ((), ())), preferred_element_type=F32)


def _mod_kernel(c_ref, w_ref, b_ref, o_ref):
    a = _silu(c_ref[...]).astype(BF16)
    o_ref[...] = _dot(a, w_ref[...].astype(BF16)) + b_ref[...]


def _modulation(cond8, w_mod, b_mod):
    tn = 1024
    return pl.pallas_call(
        _mod_kernel,
        grid=(DEPTH, N_MOD // tn),
        in_specs=[
            pl.BlockSpec((SUBLANES, D_MODEL), lambda l, j: (0, 0)),
            pl.BlockSpec((None, D_MODEL, tn), lambda l, j: (l, 0, j)),
            pl.BlockSpec((None, 1, tn), lambda l, j: (l, 0, j)),
        ],
        out_specs=pl.BlockSpec((None, SUBLANES, tn), lambda l, j: (l, 0, j)),
        out_shape=jax.ShapeDtypeStruct((DEPTH, SUBLANES, N_MOD), F32),
        compiler_params=_cparams(("parallel", "parallel"), 40),
        name="modulation",
    )(cond8, w_mod, b_mod.reshape(DEPTH, 1, N_MOD))


NORM_ROWS = 2 * SUBLANES


def _modulated_norm(x, gs, sh):
    ms = jnp.mean(x * x, axis=-1, keepdims=True)
    return x * lax.rsqrt(ms + EPS) * gs + sh


def _row_chunks(n_rows, body):
    for r0 in range(0, n_rows, NORM_ROWS):
        body(r0)


def _norm_residual(y_ref, x_ref, o_ref, gg, n_rows):
    def body(r0):
        rows = pl.ds(r0, NORM_ROWS)
        y = y_ref[rows, :]
        ms = jnp.mean(y * y, axis=-1, keepdims=True)
        o_ref[rows, :] = x_ref[rows, :] + y * lax.rsqrt(ms + EPS) * gg
    _row_chunks(n_rows, body)


def _in_proj_kernel(x_ref, g_ref, sc_ref, sh_ref, w_ref, wt_ref, o_ref, h_ref, *, tm):
    j = pl.program_id(1)

    @pl.when(j == 0)
    def _():
        gs = g_ref[...] * (1.0 + sc_ref[...])
        sh = sh_ref[...]

        def body(r0):
            rows = pl.ds(r0, NORM_ROWS)
            h_ref[rows, :] = _modulated_norm(x_ref[rows, :], gs, sh).astype(BF16)
        _row_chunks(tm, body)

    @pl.when(j < N_MAIN_SEG)
    def _():
        o_ref[...] = _dot_nt(h_ref[...], w_ref[...])

    @pl.when(j == N_MAIN_SEG)
    def _():
        o_ref[...] = _dot_nt(h_ref[...], wt_ref[...])


def _mod_spec(which, row_of_tile):
    return pl.BlockSpec((None, None, 1, D_MODEL), lambda i, *_: (row_of_tile(i), which, 0, 0))


def _gain_spec(k):
    return pl.BlockSpec((None, 1, D_MODEL), lambda i, *_: (k, 0, 0))


def _in_proj(x, gains, mod, w_in16, w_tail, layer, row_of_tile, tm):
    t = x.shape[0]
    return pl.pallas_call(
        functools.partial(_in_proj_kernel, tm=tm),
        grid=(t // tm, Z_W // SEG),
        in_specs=[
            pl.BlockSpec((tm, D_MODEL), lambda i, j: (i, 0)),
            _gain_spec(0),
            _mod_spec(1, row_of_tile),
            _mod_spec(0, row_of_tile),
            pl.BlockSpec((None, SEG, D_MODEL), lambda i, j: (layer, jnp.minimum(j, N_MAIN_SEG - 1), 0)),
            pl.BlockSpec((None, SEG, D_MODEL), lambda i, j: (layer, 0, 0)),
        ],
        out_specs=pl.BlockSpec((tm, SEG), lambda i, j: (i, j)),
        out_shape=jax.ShapeDtypeStruct((t, Z_W), F32),
        scratch_shapes=[pltpu.VMEM((tm, D_MODEL), BF16)],
        compiler_params=_cparams(("parallel", "arbitrary"), 48),
        name="in_proj",
    )(x, gains, mod, mod, w_in16, w_tail)


def _rope_tables(n):
    rows = n // GRID_W
    row = jnp.repeat(jnp.arange(rows), GRID_W)
    col = jnp.arange(rows * GRID_W) % GRID_W
    n_pair = DH_A // 4
    inv = ROPE_THETA ** (-jnp.arange(n_pair, dtype=F32) / n_pair)
    ang = jnp.concatenate([row[:, None] * inv, col[:, None] * inv], axis=-1)
    cos = jnp.repeat(jnp.cos(ang), 2, axis=-1)
    sin = jnp.repeat(jnp.sin(ang), 2, axis=-1) * jnp.tile(jnp.array([-1.0, 1.0], F32), DH_A // 2)
    return jnp.tile(cos, (1, 2)), jnp.tile(sin, (1, 2))


def _rope_kernel(q_ref, k_ref, v_ref, cos_ref, sin_ref, qo_ref, ko_ref, vo_ref):
    cos = cos_ref[...]
    sin = sin_ref[...]
    even = (lax.broadcasted_iota(jnp.int32, cos.shape, 1) % 2) == 0
    for h in range(H_A):
        sl = slice(h * LANES, (h + 1) * LANES)
        for src, dst, scale in ((q_ref, qo_ref, Q_SCALE), (k_ref, ko_ref, None)):
            x = src[:, sl]
            swapped = jnp.where(even, pltpu.roll(x, LANES - 1, 1), pltpu.roll(x, 1, 1))
            y = x * cos + swapped * sin
            if scale is not None:
                y = y * scale
            dst[:, sl] = y.astype(BF16)
        vo_ref[sl, :] = v_ref[:, sl].T.astype(BF16)


def _rope_prep(z, seqlen):
    t = z.shape[0]
    tm = 512
    cos, sin = _rope_tables(seqlen)
    per_seq = seqlen // tm
    tab_spec = pl.BlockSpec((tm, LANES), lambda i: (i % per_seq, 0))
    out = jax.ShapeDtypeStruct((t, SEG), BF16)
    return pl.pallas_call(
        _rope_kernel,
        grid=(t // tm,),
        in_specs=[pl.BlockSpec((tm, SEG), lambda i: (i, 0)),
                  pl.BlockSpec((tm, SEG), lambda i: (i, 1)),
                  pl.BlockSpec((tm, SEG), lambda i: (i, 2)),
                  tab_spec, tab_spec],
        out_specs=[pl.BlockSpec((tm, SEG), lambda i: (i, 0)), pl.BlockSpec((tm, SEG), lambda i: (i, 0)),
                   pl.BlockSpec((SEG, tm), lambda i: (0, i))],
        out_shape=[out, out, jax.ShapeDtypeStruct((SEG, t), BF16)],
        compiler_params=_cparams(("parallel",), 32),
        name="rope_prep",
    )(z, z, z, cos, sin)


def _attn_kernel(*refs, n_pieces, n_alias, cache_slot, q_prepped, v_transposed, head_major, lam_init, sub, hps):
    lq_ref, gd_ref, q_ref = refs[:3]
    kv_refs = refs[3:3 + 2 * n_pieces]
    o_ref = refs[3 + 2 * n_pieces + n_alias]
    lq = lq_ref[...]
    lam = (jnp.exp(jnp.sum(lq[0:1] * lq[1:2], axis=-1, keepdims=True))
           - jnp.exp(jnp.sum(lq[2:3] * lq[3:4], axis=-1, keepdims=True)) + lam_init)
    lo = lax.broadcasted_iota(jnp.int32, (sub, LANES), 1) < DH_A
    ones_rows = 2 * SUBLANES

    def head_cols(hh):
        return slice(hh * LANES, (hh + 1) * LANES)

    def keys(hh, p):
        k_ref = kv_refs[2 * p]
        return (k_ref[hh] if head_major[p] else k_ref[:, head_cols(hh)]).astype(BF16)

    def v_aug(hh, p):
        v_ref = kv_refs[2 * p + 1]
        if v_transposed[p]:
            vt = v_ref[head_cols(hh), :]
        else:
            vt = (v_ref[hh] if head_major[p] else v_ref[:, head_cols(hh)]).astype(F32).T
        vt = vt.astype(BF16)
        return jnp.concatenate([vt, jnp.ones((ones_rows, vt.shape[1]), BF16)], axis=0)

    def qk(u):
        hh, r = u
        q = q_ref[r * sub:(r + 1) * sub, head_cols(hh)]
        if not q_prepped:
            q = q * Q_SCALE
        zero = jnp.zeros_like(q)
        qq = jnp.concatenate([jnp.where(lo, q, zero), jnp.where(lo, zero, q)], axis=0).astype(BF16)
        return [_dot_nt(keys(hh, p), qq) for p in range(n_pieces)]

    def finish(u, scores, vts):
        hh, r = u
        m = scores[0].max(axis=0, keepdims=True)
        for s in scores[1:]:
            m = jnp.maximum(m, s.max(axis=0, keepdims=True))
        acc = None
        for p in range(n_pieces):
            e = jnp.exp2((scores[p] - m).astype(BF16))
            t = _dot(vts[p], e)
            acc = t if acc is None else acc + t
        l = acc[DV_A:DV_A + 1, :]
        ot = acc[:DV_A, :sub] * (1.0 / l[:, :sub]) - acc[:DV_A, sub:] * (lam / l[:, sub:])
        o = ot.T
        ms = jnp.mean(o * o, axis=-1, keepdims=True)
        o = o * lax.rsqrt(ms + EPS) * gd_ref[...] * (1.0 - lam_init)
        o_ref[r * sub:(r + 1) * sub, head_cols(hh)] = o.astype(o_ref.dtype)

    if cache_slot is not None:
        for c_ref, src in zip(refs[4 + 2 * n_pieces + n_alias:], kv_refs[:2]):
            dst = c_ref if n_alias else c_ref.at[cache_slot]
            for hh in range(hps):
                dst[hh] = src[:, head_cols(hh)]
            if not n_alias:
                for other in range(c_ref.shape[0]):
                    if other != cache_slot:
                        c_ref[other] = jnp.zeros(c_ref.shape[1:], c_ref.dtype)

    nsub = q_ref.shape[0] // sub
    units = [(hh, r) for hh in range(hps) for r in range(nsub)]
    nxt = qk(units[0])
    vts = None
    for n, u in enumerate(units):
        cur = nxt
        if u[1] == 0:
            vts = [v_aug(u[0], p) for p in range(n_pieces)]
        if n + 1 < len(units):
            nxt = qk(units[n + 1])
        finish(u, cur, vts)


def _attention(q_arr, pieces, v_transposed, head_major, lambda_qk_l, g_diff_l, n_seq, seqlen, tq, hps, q_prepped,
               lam_init, cache_out=None):
    per_seq = seqlen // tq
    width = hps * LANES
    in_specs = [
        pl.BlockSpec((4, DH_A), lambda b, h, i: (0, 0)),
        pl.BlockSpec((1, DV_A), lambda b, h, i: (0, 0)),
        pl.BlockSpec((tq, width), lambda b, h, i: (b * per_seq + i, h)),
    ]
    args = [lambda_qk_l, g_diff_l.reshape(1, DV_A), q_arr]
    for k_arr, v_arr, k_spec, v_spec in pieces:
        in_specs += [k_spec, v_spec]
        args += [k_arr, v_arr]
    out_specs = [pl.BlockSpec((tq, width), lambda b, h, i: (b * per_seq + i, h))]
    out_shape = [jax.ShapeDtypeStruct((n_seq * seqlen, W_A), BF16)]
    aliases = {}
    n_alias = 0
    if cache_out is not None:
        layer, caches = cache_out
        assert hps == H_A and per_seq == 1
        out_shape += [jax.ShapeDtypeStruct((n_seq, DEPTH, H_A, seqlen, LANES), F32)] * 2
        if caches is None:
            out_specs += [pl.BlockSpec((None, DEPTH, H_A, seqlen, LANES), lambda b, h, i: (b, 0, 0, 0, 0))] * 2
        else:
            out_specs += [pl.BlockSpec((None, None, H_A, seqlen, LANES), lambda b, h, i: (b, layer, 0, 0, 0))] * 2
            n_alias = 2
            aliases = {len(args): 1, len(args) + 1: 2}
            in_specs += [pl.BlockSpec(memory_space=pl.ANY)] * 2
            args += list(caches)
    return pl.pallas_call(
        functools.partial(_attn_kernel, n_pieces=len(pieces), n_alias=n_alias,
                          cache_slot=None if cache_out is None else cache_out[0],
                          q_prepped=q_prepped, v_transposed=v_transposed, head_major=head_major, lam_init=lam_init,
                          sub=128, hps=hps),
        grid=(n_seq, H_A // hps, per_seq),
        in_specs=in_specs,
        out_specs=out_specs,
        out_shape=out_shape,
        input_output_aliases=aliases,
        compiler_params=_cparams(("parallel", "parallel", "arbitrary"), 48),
        name="diff_attention",
    )(*args)


def _pool_kernel(u_ref, w_ref, sc_ref, o_ref, *, seqlen):
    rows = u_ref.shape[0]
    pos = lax.broadcasted_iota(jnp.int32, (rows, POOL_CH), 0) % seqlen
    for g, win in enumerate(POOL_WINDOWS):
        sl = slice(g * POOL_CH, (g + 1) * POOL_CH)
        u = u_ref[:, sl]
        half = win // 2
        acc = u
        for d in range(-half, half):
            if d == 0:
                continue
            shifted = pltpu.roll(u, (-d) % rows, 0)
            valid = (pos + d >= 0) & (pos + d < seqlen)
            acc = acc + jnp.where(valid, shifted, 0.0)
        cnt = (jnp.minimum(pos + half, seqlen) - jnp.maximum(pos - half, 0)).astype(F32)
        y = acc / cnt - u
        o_ref[:, sl] = (_dot(y.astype(BF16), w_ref[g].astype(BF16)) * sc_ref[:, sl]).astype(BF16)


def _pool(z, w_pool_l, pool_scale_l, seqlen, rows):
    t = z.shape[0]
    col_block = POOL_COL // W_C
    assert col_block * W_C == POOL_COL
    return pl.pallas_call(
        functools.partial(_pool_kernel, seqlen=seqlen),
        grid=(t // rows,),
        in_specs=[pl.BlockSpec((rows, W_C), lambda i: (i, col_block)),
                  pl.BlockSpec((N_POOL, POOL_CH, POOL_CH), lambda i: (0, 0, 0)),
                  pl.BlockSpec((1, W_C), lambda i: (0, 0))],
        out_specs=pl.BlockSpec((rows, W_C), lambda i: (i, 0)),
        out_shape=jax.ShapeDtypeStruct((t, W_C), BF16),
        compiler_params=_cparams(("parallel",), 48),
        name="pool_mixer",
    )(z, w_pool_l, pool_scale_l.reshape(1, W_C))


def _halo_specs(tm, width, col_block, t):
    nb = t // SUBLANES
    per = tm // SUBLANES
    prev = pl.BlockSpec((SUBLANES, width), lambda i, *_: (jnp.maximum(i * per - 1, 0), col_block))
    nxt = pl.BlockSpec((SUBLANES, width), lambda i, *_: (jnp.minimum((i + 1) * per, nb - 1), col_block))
    return prev, nxt


def _replace_rows(a, fixes):
    sub = lax.broadcasted_iota(jnp.int32, (SUBLANES, a.shape[1]), 0)
    parts, cur = [], 0
    for row, val in sorted(fixes, key=lambda f: f[0]):
        s0 = row // SUBLANES * SUBLANES
        if s0 > cur:
            parts.append(a[cur:s0])
        parts.append(jnp.where(sub == row % SUBLANES, val, a[s0:s0 + SUBLANES]))
        cur = s0 + SUBLANES
    if cur < a.shape[0]:
        parts.append(a[cur:])
    return jnp.concatenate(parts, axis=0)


def _dn_prep_kernel(x_ref, xprev_ref, xnext_ref, gate_ref, cw_ref, gp_ref, qkv_ref, gb_ref, *, seqlen, tm):
    i = pl.program_id(0)
    cw = cw_ref[...]
    cblk = 256
    starts = list(range(0, tm, seqlen)) if seqlen < tm else [0]
    ends = [r + seqlen - 1 for r in range(0, tm, seqlen)] if seqlen < tm else [tm - 1]
    prev_ok = (i * tm) % seqlen != 0 if seqlen > tm else False
    next_ok = ((i + 1) * tm) % seqlen != 0 if seqlen > tm else False
    zero_row = jnp.zeros((1, cblk), F32)
    for cb in range(QKVB_W // cblk):
        sl = slice(cb * cblk, (cb + 1) * cblk)
        x = x_ref[:, sl]
        prev_row = jnp.where(prev_ok, xprev_ref[SUBLANES - 1:SUBLANES, sl], zero_row)
        next_row = jnp.where(next_ok, xnext_ref[0:1, sl], zero_row)
        xm = _replace_rows(pltpu.roll(x, 1, 0), [(r, prev_row if r == 0 else zero_row) for r in starts])
        xp = _replace_rows(pltpu.roll(x, tm - 1, 0), [(r, next_row if r == tm - 1 else zero_row) for r in ends])
        y = xm * cw[0:1, sl] + x * cw[1:2, sl] + xp * cw[2:3, sl]
        y = y * (1.0 / (1.0 + jnp.exp2(y * -LOG2E)))
        for hh in range(cblk // LANES):
            c0 = cb * cblk + hh * LANES
            yh = y[:, hh * LANES:(hh + 1) * LANES]
            if c0 < 2 * H_B * DK_B:
                inv = lax.rsqrt(jnp.sum(yh * yh, axis=-1, keepdims=True) + EPS)
                yh = yh * (inv * (DK_B ** -0.5) if c0 < H_B * DK_B else inv)
            qkv_ref[:, c0:c0 + LANES] = yh
    lane = lax.broadcasted_iota(jnp.int32, (tm, LANES), 1)
    for d in range(2):
        raw = gate_ref[:, d * LANES:(d + 1) * LANES]
        gp = gp_ref[d]
        beta = 1.0 / (1.0 + jnp.exp(-raw))
        xa = raw + gp[1:2]
        softplus = jnp.maximum(xa, 0.0) + jnp.log1p(jnp.exp(-jnp.abs(xa)))
        gdec = -jnp.exp(gp[0:1]) * softplus
        gb_ref[d] = jnp.where(lane < H_B, beta, jnp.where(lane < 2 * H_B, gdec, 0.0))


def _dn_prep(z, conv_qkv_l, gate_params, seqlen, tm):
    t = z.shape[0]
    prev, nxt = _halo_specs(tm, QKVB_W, 1, t)
    return pl.pallas_call(
        functools.partial(_dn_prep_kernel, seqlen=seqlen, tm=tm),
        grid=(t // tm,),
        in_specs=[pl.BlockSpec((tm, QKVB_W), lambda i: (i, 1)), prev, nxt,
                  pl.BlockSpec((tm, 2 * LANES), lambda i: (i, GATE_COL // (2 * LANES))),
                  pl.BlockSpec((3, QKVB_W), lambda i: (0, 0)),
                  pl.BlockSpec((2, 2, LANES), lambda i: (0, 0, 0))],
        out_specs=[pl.BlockSpec((tm, QKVB_W), lambda i: (i, 0)),
                   pl.BlockSpec((2, tm, LANES), lambda i: (0, i, 0))],
        out_shape=[jax.ShapeDtypeStruct((t, QKVB_W), F32), jax.ShapeDtypeStruct((2, t, LANES), F32)],
        compiler_params=_cparams(("parallel",), 48),
        name="deltanet_prep",
    )(z, z, z, z, conv_qkv_l, gate_params)


def _dn_scan_kernel(*refs, cs, has_s0, state_slot):
    qkv_refs, gb_refs = refs[0:2], refs[2:4]
    s0_ref = refs[4] if has_s0 else None
    o_refs, sfin_ref, s_scr = refs[-4:-2], refs[-2], refs[-1]
    st = pl.program_id(1)

    @pl.when(st == 0)
    def _():
        if has_s0:
            s_scr[...] = s0_ref[...]
        else:
            s_scr[...] = jnp.zeros_like(s_scr)

    ii = lax.broadcasted_iota(jnp.int32, (CHUNK, CHUNK), 0)
    jj = lax.broadcasted_iota(jnp.int32, (CHUNK, CHUNK), 1)
    incl_d = (ii >= jj, ii <= jj)
    strict_d = (ii > jj, ii < jj)
    eye = (ii == jj).astype(F32)
    eq_masks = [((ii >> (3 + k)) == (jj >> (3 + k))).astype(F32) for k in range(3)] + [jnp.ones((CHUNK, CHUNK), F32)]
    offs = []
    st_q, st_k, st_v, st_beta, st_gc, st_gr, st_gt, st_d = [], [], [], [], [], [], [], []
    for d in range(2):
        tri = incl_d[d].astype(F32)
        for pos in range(cs):
            off = (pos if d == 0 else cs - 1 - pos) * CHUNK
            offs.append(off)
            gbc = gb_refs[d][off:off + CHUNK, :]
            gcol = jnp.dot(tri, gbc, preferred_element_type=F32, precision=lax.Precision.HIGHEST)
            grow = gcol.T
            gtot = jnp.sum(gbc, axis=0, keepdims=True)
            for h in range(H_B):
                st_d.append(d)
                st_q.append(qkv_refs[d][off:off + CHUNK, h * DK_B:(h + 1) * DK_B])
                st_k.append(qkv_refs[d][off:off + CHUNK, (H_B + h) * DK_B:(H_B + h + 1) * DK_B])
                st_v.append(qkv_refs[d][off:off + CHUNK, (2 * H_B + h) * DK_B:(2 * H_B + h + 1) * DK_B])
                st_beta.append(gbc[:, h:h + 1])
                st_gc.append(gcol[:, H_B + h:H_B + h + 1])
                st_gr.append(grow[H_B + h:H_B + h + 1, :])
                st_gt.append(gtot[:, H_B + h:H_B + h + 1])
    rng = range(2 * cs * H_B)
    st_eg = [jnp.exp(st_gc[i]) for i in rng]
    st_kq = [_dot_nt(jnp.concatenate([st_k[i], st_q[i]], axis=0).astype(BF16), st_k[i].astype(BF16)) for i in rng]
    st_lmat, st_a, st_x0 = [], [], []
    for i in rng:
        incl, strict = incl_d[st_d[i]], strict_d[st_d[i]]
        dec = jnp.where(incl, jnp.exp(jnp.where(incl, st_gc[i] - st_gr[i], 0.0)), 0.0)
        st_lmat.append(jnp.where(strict, st_kq[i][:CHUNK] * dec, 0.0) * st_beta[i])
        st_a.append((st_kq[i][CHUNK:] * dec).astype(BF16))
        st_x0.append(jnp.concatenate([st_v[i] * st_beta[i], st_k[i] * (st_beta[i] * st_eg[i])], axis=1))
    st_lbf = [st_lmat[i] * eq_masks[0] for i in rng]
    st_lb = [st_lbf[i].astype(BF16) for i in rng]
    st_p2f = [_dot(st_lb[i], st_lb[i]) for i in rng]
    st_p2 = [st_p2f[i].astype(BF16) for i in rng]
    st_p3 = [_dot(st_lb[i], st_p2[i]) for i in rng]
    st_p4 = [_dot(st_p2[i], st_p2[i]).astype(BF16) for i in rng]
    st_t = [eye - st_lbf[i] + st_p2f[i] - st_p3[i] for i in rng]
    st_t = [st_t[i] + _dot(st_t[i].astype(BF16), st_p4[i]) for i in rng]
    for lvl in range(3):
        cmask = eq_masks[lvl + 1] - eq_masks[lvl]
        st_t16 = [st_t[i].astype(BF16) for i in rng]
        st_m = [_dot((st_lmat[i] * cmask).astype(BF16), st_t16[i]).astype(BF16) for i in rng]
        st_t = [st_t[i] - _dot(st_t16[i], st_m[i]) for i in rng]
    st_x = [st_x0[i] + _dot((st_t[i] - eye).astype(BF16), st_x0[i].astype(BF16)) for i in rng]
    s_cur = [s_scr[d, h] for d in range(2) for h in range(H_B)]
    nch = range(2 * H_B)
    for pos in range(cs):
        idx = [(d * cs + pos) * H_B + h for d in range(2) for h in range(H_B)]
        wq = [jnp.concatenate([st_x[i][:, DV_B:], st_q[i] * st_eg[i]], axis=0).astype(BF16) for i in idx]
        r = [_dot(wq[c], s_cur[c].astype(BF16)) for c in nch]
        vn16 = [(st_x[i][:, :DV_B] - r[c][:CHUNK]).astype(BF16) for c, i in enumerate(idx)]
        o = [r[c][CHUNK:] + _dot(st_a[i], vn16[c]) for c, i in enumerate(idx)]
        kd16 = [(st_k[i] * jnp.exp(st_gt[i] - st_gc[i])).astype(BF16) for i in idx]
        s_cur = [s_cur[c] * jnp.exp(st_gt[i]) + _dot_tn(kd16[c], vn16[c]) for c, i in enumerate(idx)]
        for c in nch:
            d, h = divmod(c, H_B)
            off = offs[d * cs + pos]
            o_refs[d][off:off + CHUNK, h * DV_B:(h + 1) * DV_B] = o[c].astype(BF16)
    for c in nch:
        s_scr[c // H_B, c % H_B] = s_cur[c]

    @pl.when(st == pl.num_programs(1) - 1)
    def _():
        if state_slot is None:
            sfin_ref[...] = s_scr[...]
        else:
            for other in range(sfin_ref.shape[0]):
                sfin_ref[other] = s_scr[...] if other == state_slot else jnp.zeros_like(s_scr)


def _dn_scan(qkv, gb, s0, layer, n_seq, seqlen, cs, state_out=None):
    t = qkv.shape[0]
    rows = cs * CHUNK
    nsteps = seqlen // rows
    fwd = lambda b, s: b * nsteps + s
    bwd = lambda b, s: b * nsteps + nsteps - 1 - s
    in_specs = [pl.BlockSpec((rows, QKVB_W), lambda b, s: (fwd(b, s), 0)),
                pl.BlockSpec((rows, QKVB_W), lambda b, s: (bwd(b, s), 0)),
                pl.BlockSpec((None, rows, LANES), lambda b, s: (0, fwd(b, s), 0)),
                pl.BlockSpec((None, rows, LANES), lambda b, s: (1, bwd(b, s), 0))]
    args = [qkv, qkv, gb, gb]
    if s0 is not None:
        in_specs.append(pl.BlockSpec((None, None, 2, H_B, DK_B, DV_B), lambda b, s: (b, layer, 0, 0, 0, 0)))
        args.append(s0)
    o_shape = jax.ShapeDtypeStruct((t, W_B), BF16)
    aliases = {}
    state_slot = None
    if state_out is None:
        s_spec = pl.BlockSpec((None, 2, H_B, DK_B, DV_B), lambda b, s: (b, 0, 0, 0, 0))
        s_shape = jax.ShapeDtypeStruct((n_seq, 2, H_B, DK_B, DV_B), F32)
    else:
        s_layer, s_prev = state_out
        s_shape = jax.ShapeDtypeStruct((n_seq, DEPTH, 2, H_B, DK_B, DV_B), F32)
        if s_prev is None:
            state_slot = s_layer
            s_spec = pl.BlockSpec((None, DEPTH, 2, H_B, DK_B, DV_B), lambda b, s: (b, 0, 0, 0, 0, 0))
        else:
            s_spec = pl.BlockSpec((None, None, 2, H_B, DK_B, DV_B), lambda b, s: (b, s_layer, 0, 0, 0, 0))
            aliases = {len(args): 2}
            in_specs.append(pl.BlockSpec(memory_space=pl.ANY))
            args.append(s_prev)
    return pl.pallas_call(
        functools.partial(_dn_scan_kernel, cs=cs, has_s0=s0 is not None, state_slot=state_slot),
        grid=(n_seq, nsteps),
        in_specs=in_specs,
        out_specs=[pl.BlockSpec((rows, W_B), lambda b, s: (fwd(b, s), 0)),
                   pl.BlockSpec((rows, W_B), lambda b, s: (bwd(b, s), 0)),
                   s_spec],
        out_shape=[o_shape, o_shape, s_shape],
        input_output_aliases=aliases,
        scratch_shapes=[pltpu.VMEM((2, H_B, DK_B, DV_B), F32)],
        compiler_params=_cparams(("parallel", "arbitrary"), 40),
        name="deltanet_scan",
    )(*args)


def _out_proj_kernel(oa_ref, of_ref, ob_ref, z_ref, oc_ref, x_ref, gt_ref, g1_ref, gd_ref, w_ref, o_ref, mix_ref):
    mix_ref[:, 0:W_A] = oa_ref[...]
    gd = gd_ref[...]
    for h in range(H_B):
        sl = slice(h * DV_B, (h + 1) * DV_B)
        ob = of_ref[:, sl].astype(F32) + ob_ref[:, sl].astype(F32)
        ms = jnp.mean(ob * ob, axis=-1, keepdims=True)
        ob = ob * lax.rsqrt(ms + EPS) * gd * _silu(z_ref[:, sl])
        mix_ref[:, W_A + h * DV_B:W_A + (h + 1) * DV_B] = ob.astype(BF16)
    mix_ref[:, W_A + W_B:] = oc_ref[...]
    o_ref[...] = _dot(mix_ref[...], w_ref[...])
    _norm_residual(o_ref, x_ref, o_ref, gt_ref[...] * g1_ref[...], o_ref.shape[0])


def _out_proj(oa, o_f, o_b, z, oc, x, gains, mod, g_delta_l, w_out16, layer, row_of_tile, tm):
    t = x.shape[0]
    return pl.pallas_call(
        _out_proj_kernel,
        grid=(t // tm,),
        in_specs=[pl.BlockSpec((tm, W_A), lambda i: (i, 0)),
                  pl.BlockSpec((tm, W_B), lambda i: (i, 0)),
                  pl.BlockSpec((tm, W_B), lambda i: (i, 0)),
                  pl.BlockSpec((tm, SEG), lambda i: (i, ZB_SEG)),
                  pl.BlockSpec((tm, W_C), lambda i: (i, 0)),
                  pl.BlockSpec((tm, D_MODEL), lambda i: (i, 0)),
                  _mod_spec(2, row_of_tile),
                  _gain_spec(1),
                  pl.BlockSpec((1, DV_B), lambda i: (0, 0)),
                  pl.BlockSpec((None, D_MODEL, D_MODEL), lambda i: (layer, 0, 0))],
        out_specs=pl.BlockSpec((tm, D_MODEL), lambda i: (i, 0)),
        out_shape=jax.ShapeDtypeStruct((t, D_MODEL), F32),
        scratch_shapes=[pltpu.VMEM((tm, D_MODEL), BF16)],
        compiler_params=_cparams(("parallel",), 56),
        name="out_proj",
    )(oa, o_f, o_b, z, oc, x, mod, gains, g_delta_l.reshape(1, DV_B), w_out16)


def _ffn_kernel(x_ref, xprev_ref, xnext_ref, g2_ref, sc_ref, sh_ref, gt_ref, g3_ref,
                wg_ref, wu_ref, cg_ref, cu_ref, wd_ref, o_ref, h_ref, p_ref, acc_ref, *, seqlen, tm):
    i = pl.program_id(0)
    j = pl.program_id(1)
    hb = SUBLANES

    @pl.when(j == 0)
    def _():
        gs = g2_ref[...] * (1.0 + sc_ref[...])
        sh = sh_ref[...]
        norm = lambda v: _modulated_norm(v, gs, sh).astype(BF16)
        h_ref[0:hb, :] = norm(xprev_ref[...])
        for r0 in range(0, tm, NORM_ROWS):
            h_ref[hb + r0:hb + r0 + NORM_ROWS, :] = norm(x_ref[r0:r0 + NORM_ROWS, :])
        h_ref[hb + tm:, :] = norm(xnext_ref[...])
        acc_ref[...] = jnp.zeros_like(acc_ref)

    tf = wg_ref.shape[1]
    pos = (i * tm + lax.broadcasted_iota(jnp.int32, (tm, tf), 0)) % seqlen
    has_prev = pos != 0
    has_next = pos != seqlen - 1

    def conv_branch(w_ref, c_ref):
        p_ref[...] = _dot(h_ref[...], w_ref[...])
        cw = c_ref[...]
        return (jnp.where(has_prev, p_ref[hb - 1:hb - 1 + tm, :], 0.0) * cw[0:1]
                + p_ref[hb:hb + tm, :] * cw[1:2]
                + jnp.where(has_next, p_ref[hb + 1:hb + 1 + tm, :], 0.0) * cw[2:3])

    gate = conv_branch(wg_ref, cg_ref)
    up = conv_branch(wu_ref, cu_ref)
    acc_ref[...] += _dot((_silu(gate) * up).astype(BF16), wd_ref[...])

    @pl.when(j == pl.num_programs(1) - 1)
    def _():
        _norm_residual(acc_ref, x_ref, o_ref, gt_ref[...] * g3_ref[...], tm)


def _ffn(x, gains, mod, w_up16, conv_ffn, w_down16, layer, row_of_tile, seqlen, tm, tf):
    t = x.shape[0]
    nf = D_FF // tf
    prev, nxt = _halo_specs(tm, D_MODEL, 0, t)
    return pl.pallas_call(
        functools.partial(_ffn_kernel, seqlen=seqlen, tm=tm),
        grid=(t // tm, nf),
        in_specs=[pl.BlockSpec((tm, D_MODEL), lambda i, j: (i, 0)), prev, nxt,
                  _gain_spec(2), _mod_spec(4, row_of_tile), _mod_spec(3, row_of_tile), _mod_spec(5, row_of_tile),
                  _gain_spec(3),
                  pl.BlockSpec((None, D_MODEL, tf), lambda i, j: (layer, 0, j)),
                  pl.BlockSpec((None, D_MODEL, tf), lambda i, j: (layer, 0, nf + j)),
                  pl.BlockSpec((None, 3, tf), lambda i, j: (layer, 0, j)),
                  pl.BlockSpec((None, 3, tf), lambda i, j: (layer, 0, nf + j)),
                  pl.BlockSpec((None, tf, D_MODEL), lambda i, j: (layer, j, 0))],
        out_specs=pl.BlockSpec((tm, D_MODEL), lambda i, j: (i, 0)),
        out_shape=jax.ShapeDtypeStruct((t, D_MODEL), F32),
        scratch_shapes=[pltpu.VMEM((tm + 2 * SUBLANES, D_MODEL), BF16),
                        pltpu.VMEM((tm + 2 * SUBLANES, tf), F32),
                        pltpu.VMEM((tm, D_MODEL), F32)],
        compiler_params=_cparams(("parallel", "arbitrary"), 56),
        name="conv_ffn",
    )(x, x, x, gains, mod, mod, mod, gains, w_up16, w_up16, conv_ffn, conv_ffn, w_down16)


def _pack_w_tail(w_in_t):
    tail = w_in_t[:, N_MAIN_SEG * SEG:, :]
    beta = tail[:, :2 * H_B]
    alpha = tail[:, 2 * H_B:4 * H_B]
    pool = tail[:, 4 * H_B:]
    pad = jnp.zeros((DEPTH, LANES - 2 * H_B, D_MODEL), w_in_t.dtype)
    rows = []
    for d in range(2):
        rows += [beta[:, d * H_B:(d + 1) * H_B], alpha[:, d * H_B:(d + 1) * H_B], pad]
    rows.append(pool)
    return jnp.concatenate(rows, axis=1).astype(BF16)


def _gate_params(a_log_l, dt_bias_l):
    gp = jnp.zeros((2, 2, LANES), F32)
    gp = gp.at[:, 0, H_B:2 * H_B].set(a_log_l.astype(F32))
    return gp.at[:, 1, H_B:2 * H_B].set(dt_bias_l.astype(F32))


def _layer(x, mod_l, row_of_tile_fn, p, l, n_seq, seqlen, ctx, new_caches=None):
    t = x.shape[0]
    gains = p['norm_gains'][l].reshape(4, 1, D_MODEL)
    lam_init = 0.8 - 0.6 * math.exp(-0.3 * l)
    tm_big = 1024
    assert t % tm_big == 0 and (seqlen % tm_big == 0 or tm_big % seqlen == 0)
    z = _in_proj(x, gains, mod_l, p['w_in16'], p['w_tail'], l, row_of_tile_fn(tm_big), tm_big)

    if ctx is None:
        kv_spec = lambda seg: pl.BlockSpec((seqlen, W_A), lambda b, h, i: (b, seg))
        pieces = [(z, z, kv_spec(1), kv_spec(2))]
        kv_prev = None if new_caches is None else new_caches[:2]
        oa, k_new, v_new = _attention(z, pieces, (False,), (False,), p['lambda_qk'][l], p['g_diff'][l], n_seq, seqlen,
                                      seqlen, H_A, False, lam_init, cache_out=(l, kv_prev))
    else:
        cache_k, cache_v, state_delta = ctx
        past = cache_k.shape[2]
        q_r, k_r, v_t = _rope_prep(z, seqlen)
        hps = 2
        lat_spec = pl.BlockSpec((seqlen, hps * LANES), lambda b, h, i: (b, h))
        vt_spec = pl.BlockSpec((hps * LANES, seqlen), lambda b, h, i: (h, b))
        ck = cache_k.transpose(0, 1, 3, 2, 4)
        cv = cache_v.transpose(0, 1, 3, 2, 4)
        c_spec = pl.BlockSpec((None, None, hps, past, LANES), lambda b, h, i: (b, l, h, 0, 0))
        pieces = [(k_r, v_t, lat_spec, vt_spec), (ck, cv, c_spec, c_spec)]
        oa, = _attention(q_r, pieces, (True, False), (False, True), p['lambda_qk'][l], p['g_diff'][l], n_seq, seqlen,
                         1024, hps, True, lam_init)

    qkv, gb = _dn_prep(z, p['conv_qkv'][l], _gate_params(p['a_log'][l], p['dt_bias'][l]), seqlen, tm_big)
    s_prev = None if new_caches is None else new_caches[2]
    o_f, o_b, s_fin = _dn_scan(qkv, gb, None if ctx is None else ctx[2], l, n_seq, seqlen, 4,
                               state_out=(l, s_prev) if ctx is None else None)

    oc = _pool(z, p['w_pool'][l], p['pool_scale'][l], seqlen, max(seqlen, 1024))

    tm = 512
    x = _out_proj(oa, o_f, o_b, z, oc, x, gains, mod_l, p['g_delta'][l], p['w_out16'], l, row_of_tile_fn(tm), tm)
    tm = 512
    x = _ffn(x, gains, mod_l, p['w_up16'], p['conv_ffn'], p['w_down16'], l, row_of_tile_fn(tm), seqlen, tm, 512)
    if ctx is None:
        return x, (k_new, v_new, s_fin)
    return x, None


def kernel(x_prompt, x_sample, cache_k, cache_v, state_delta, c, c_ctx, w_mod, b_mod, norm_gains, w_in, lambda_qk,
           g_diff, conv_qkv, a_log, dt_bias, g_delta, w_pool, pool_scale, w_out, w_up, conv_ffn, w_down):
    batch, seq, _ = x_prompt.shape
    dec_batch, dec_seq, _ = x_sample.shape
    w_in_t = w_in.transpose(0, 2, 1)
    p = {'norm_gains': norm_gains, 'lambda_qk': lambda_qk, 'g_diff': g_diff, 'conv_qkv': conv_qkv, 'a_log': a_log,
         'dt_bias': dt_bias, 'g_delta': g_delta, 'w_pool': w_pool, 'pool_scale': pool_scale, 'conv_ffn': conv_ffn,
         'w_in16': w_in_t.astype(BF16), 'w_tail': _pack_w_tail(w_in_t),
         'w_out16': w_out.astype(BF16), 'w_up16': w_up.astype(BF16), 'w_down16': w_down.astype(BF16)}

    assert 1 + dec_batch <= SUBLANES
    cond8 = jnp.concatenate([c_ctx[None, :], c, jnp.zeros((SUBLANES - 1 - dec_batch, D_MODEL), F32)], axis=0)
    mod = _modulation(cond8, w_mod, b_mod).reshape(DEPTH, SUBLANES, 6, 1, D_MODEL)

    ctx_rows = lambda tm: (lambda i: 0)
    lat_rows = lambda tm: (lambda i: 1 + (i * tm) // dec_seq)

    xp = x_prompt.reshape(batch * seq, D_MODEL)
    new_caches = None
    for l in range(DEPTH):
        xp, new_caches = _layer(xp, mod[l], ctx_rows, p, l, batch, seq, None, new_caches)
    xs = x_sample.reshape(dec_batch * dec_seq, D_MODEL)
    for l in range(DEPTH):
        xs, _ = _layer(xs, mod[l], lat_rows, p, l, dec_batch, dec_seq, (cache_k, cache_v, state_delta))
    return (xp.reshape(batch, seq, D_MODEL), xs.reshape(dec_batch, dec_seq, D_MODEL),
            new_caches[0].transpose(0, 1, 3, 2, 4), new_caches[1].transpose(0, 1, 3, 2, 4), new_caches[2])
```

```python
import functools
import math

import jax
import jax.numpy as jnp
from jax import lax
from jax.experimental import pallas as pl
from jax.experimental.pallas import tpu as pltpu

D_MODEL = 2048
DEPTH = 2
GRID_W = 64
H_A = 6
DH_A = 64
DV_A = 2 * DH_A
H_B = 6
DK_B = 128
DV_B = 128
CHUNK = 64
N_POOL = 4
POOL_CH = 128
POOL_WINDOWS = (2, 4, 8, 16)
W_A = H_A * DV_A
W_B = H_B * DV_B
W_C = N_POOL * POOL_CH
D_FF = 5632
ROPE_THETA = 10000.0
EPS = 1e-6
LOG2E = math.log2(math.e)
Q_SCALE = DH_A ** -0.5 * LOG2E

LANES = 128
SUBLANES = 8
SEG = 768
QKVB_W = H_B * (2 * DK_B + DV_B)
Z_W = 8 * SEG
N_MAIN_SEG = 7
ZB_SEG = 6
GATE_COL = N_MAIN_SEG * SEG
POOL_COL = GATE_COL + 2 * LANES
N_MOD = 6 * D_MODEL
VMEM_PHYS = 64 * 1024 * 1024

F32 = jnp.float32
BF16 = jnp.bfloat16


def _cparams(sem, vmem_mb):
    assert vmem_mb * 1024 * 1024 < VMEM_PHYS
    return pltpu.CompilerParams(dimension_semantics=sem, vmem_limit_bytes=vmem_mb * 1024 * 1024)


def _silu(x):
    return x * (1.0 / (1.0 + jnp.exp(-x)))


def _dot(a, b):
    return jnp.dot(a, b, preferred_element_type=F32)


def _dot_nt(a, b):
    return lax.dot_general(a, b, (((1,), (1,)), ((), ())), preferred_element_type=F32)


def _dot_tn(a, b):
    return lax.dot_general(a, b, (((0,), (0,)), ((), ())), preferred_element_type=F32)


def _mod_kernel(c_ref, w_ref, b_ref, o_ref):
    a = _silu(c_ref[...]).astype(BF16)
    o_ref[...] = _dot(a, w_ref[...].astype(BF16)) + b_ref[...]


def _modulation(cond8, w_mod, b_mod):
    tn = 1024
    return pl.pallas_call(
        _mod_kernel,
        grid=(DEPTH, N_MOD // tn),
        in_specs=[
            pl.BlockSpec((SUBLANES, D_MODEL), lambda l, j: (0, 0)),
            pl.BlockSpec((None, D_MODEL, tn), lambda l, j: (l, 0, j)),
            pl.BlockSpec((None, 1, tn), lambda l, j: (l, 0, j)),
        ],
        out_specs=pl.BlockSpec((None, SUBLANES, tn), lambda l, j: (l, 0, j)),
        out_shape=jax.ShapeDtypeStruct((DEPTH, SUBLANES, N_MOD), F32),
        compiler_params=_cparams(("parallel", "parallel"), 40),
        name="modulation",
    )(cond8, w_mod, b_mod.reshape(DEPTH, 1, N_MOD))


NORM_ROWS = 2 * SUBLANES


def _modulated_norm(x, gs, sh):
    ms = jnp.mean(x * x, axis=-1, keepdims=True)
    return x * lax.rsqrt(ms + EPS) * gs + sh


def _row_chunks(n_rows, body):
    for r0 in range(0, n_rows, NORM_ROWS):
        body(r0)


def _norm_residual(y_ref, x_ref, o_ref, gg, n_rows):
    def body(r0):
        rows = pl.ds(r0, NORM_ROWS)
        y = y_ref[rows, :]
        ms = jnp.mean(y * y, axis=-1, keepdims=True)
        o_ref[rows, :] = x_ref[rows, :] + y * lax.rsqrt(ms + EPS) * gg
    _row_chunks(n_rows, body)


def _in_proj_kernel(x_ref, g_ref, sc_ref, sh_ref, w_ref, wt_ref, o_ref, h_ref, *, tm):
    j = pl.program_id(1)

    @pl.when(j == 0)
    def _():
        gs = g_ref[...] * (1.0 + sc_ref[...])
        sh = sh_ref[...]

        def body(r0):
            rows = pl.ds(r0, NORM_ROWS)
            h_ref[rows, :] = _modulated_norm(x_ref[rows, :], gs, sh).astype(BF16)
        _row_chunks(tm, body)

    @pl.when(j < N_MAIN_SEG)
    def _():
        o_ref[...] = _dot_nt(h_ref[...], w_ref[...])

    @pl.when(j == N_MAIN_SEG)
    def _():
        o_ref[...] = _dot_nt(h_ref[...], wt_ref[...])


def _mod_spec(which, row_of_tile):
    return pl.BlockSpec((None, None, 1, D_MODEL), lambda i, *_: (row_of_tile(i), which, 0, 0))


def _gain_spec(k):
    return pl.BlockSpec((None, 1, D_MODEL), lambda i, *_: (k, 0, 0))


def _in_proj(x, gains, mod, w_in16, w_tail, layer, row_of_tile, tm):
    t = x.shape[0]
    return pl.pallas_call(
        functools.partial(_in_proj_kernel, tm=tm),
        grid=(t // tm, Z_W // SEG),
        in_specs=[
            pl.BlockSpec((tm, D_MODEL), lambda i, j: (i, 0)),
            _gain_spec(0),
            _mod_spec(1, row_of_tile),
            _mod_spec(0, row_of_tile),
            pl.BlockSpec((None, SEG, D_MODEL), lambda i, j: (layer, jnp.minimum(j, N_MAIN_SEG - 1), 0)),
            pl.BlockSpec((None, SEG, D_MODEL), lambda i, j: (layer, 0, 0)),
        ],
        out_specs=pl.BlockSpec((tm, SEG), lambda i, j: (i, j)),
        out_shape=jax.ShapeDtypeStruct((t, Z_W), F32),
        scratch_shapes=[pltpu.VMEM((tm, D_MODEL), BF16)],
        compiler_params=_cparams(("parallel", "arbitrary"), 48),
        name="in_proj",
    )(x, gains, mod, mod, w_in16, w_tail)


def _rope_tables(n):
    rows = n // GRID_W
    row = jnp.repeat(jnp.arange(rows), GRID_W)
    col = jnp.arange(rows * GRID_W) % GRID_W
    n_pair = DH_A // 4
    inv = ROPE_THETA ** (-jnp.arange(n_pair, dtype=F32) / n_pair)
    ang = jnp.concatenate([row[:, None] * inv, col[:, None] * inv], axis=-1)
    cos = jnp.repeat(jnp.cos(ang), 2, axis=-1)
    sin = jnp.repeat(jnp.sin(ang), 2, axis=-1) * jnp.tile(jnp.array([-1.0, 1.0], F32), DH_A // 2)
    return jnp.tile(cos, (1, 2)), jnp.tile(sin, (1, 2))


def _rope_kernel(q_ref, k_ref, v_ref, cos_ref, sin_ref, qo_ref, ko_ref, vo_ref):
    cos = cos_ref[...]
    sin = sin_ref[...]
    even = (lax.broadcasted_iota(jnp.int32, cos.shape, 1) % 2) == 0
    for h in range(H_A):
        sl = slice(h * LANES, (h + 1) * LANES)
        for src, dst, scale in ((q_ref, qo_ref, Q_SCALE), (k_ref, ko_ref, None)):
            x = src[:, sl]
            swapped = jnp.where(even, pltpu.roll(x, LANES - 1, 1), pltpu.roll(x, 1, 1))
            y = x * cos + swapped * sin
            if scale is not None:
                y = y * scale
            dst[:, sl] = y.astype(BF16)
        vo_ref[sl, :] = v_ref[:, sl].T.astype(BF16)


def _rope_prep(z, seqlen):
    t = z.shape[0]
    tm = 512
    cos, sin = _rope_tables(seqlen)
    per_seq = seqlen // tm
    tab_spec = pl.BlockSpec((tm, LANES), lambda i: (i % per_seq, 0))
    out = jax.ShapeDtypeStruct((t, SEG), BF16)
    return pl.pallas_call(
        _rope_kernel,
        grid=(t // tm,),
        in_specs=[pl.BlockSpec((tm, SEG), lambda i: (i, 0)),
                  pl.BlockSpec((tm, SEG), lambda i: (i, 1)),
                  pl.BlockSpec((tm, SEG), lambda i: (i, 2)),
                  tab_spec, tab_spec],
        out_specs=[pl.BlockSpec((tm, SEG), lambda i: (i, 0)), pl.BlockSpec((tm, SEG), lambda i: (i, 0)),
                   pl.BlockSpec((SEG, tm), lambda i: (0, i))],
        out_shape=[out, out, jax.ShapeDtypeStruct((SEG, t), BF16)],
        compiler_params=_cparams(("parallel",), 32),
        name="rope_prep",
    )(z, z, z, cos, sin)


def _attn_kernel(*refs, n_pieces, n_alias, cache_slot, q_prepped, v_transposed, head_major, lam_init, sub, hps):
    lq_ref, gd_ref, q_ref = refs[:3]
    kv_refs = refs[3:3 + 2 * n_pieces]
    o_ref = refs[3 + 2 * n_pieces + n_alias]
    lq = lq_ref[...]
    lam = (jnp.exp(jnp.sum(lq[0:1] * lq[1:2], axis=-1, keepdims=True))
           - jnp.exp(jnp.sum(lq[2:3] * lq[3:4], axis=-1, keepdims=True)) + lam_init)
    lo = lax.broadcasted_iota(jnp.int32, (sub, LANES), 1) < DH_A
    ones_rows = 2 * SUBLANES

    def head_cols(hh):
        return slice(hh * LANES, (hh + 1) * LANES)

    def keys(hh, p):
        k_ref = kv_refs[2 * p]
        return (k_ref[hh] if head_major[p] else k_ref[:, head_cols(hh)]).astype(BF16)

    def v_aug(hh, p):
        v_ref = kv_refs[2 * p + 1]
        if v_transposed[p]:
            vt = v_ref[head_cols(hh), :]
        else:
            vt = (v_ref[hh] if head_major[p] else v_ref[:, head_cols(hh)]).astype(F32).T
        vt = vt.astype(BF16)
        return jnp.concatenate([vt, jnp.ones((ones_rows, vt.shape[1]), BF16)], axis=0)

    def qk(u):
        hh, r = u
        q = q_ref[r * sub:(r + 1) * sub, head_cols(hh)]
        if not q_prepped:
            q = q * Q_SCALE
        zero = jnp.zeros_like(q)
        qq = jnp.concatenate([jnp.where(lo, q, zero), jnp.where(lo, zero, q)], axis=0).astype(BF16)
        return [_dot_nt(keys(hh, p), qq) for p in range(n_pieces)]

    def finish(u, scores, vts):
        hh, r = u
        m = scores[0].max(axis=0, keepdims=True)
        for s in scores[1:]:
            m = jnp.maximum(m, s.max(axis=0, keepdims=True))
        acc = None
        for p in range(n_pieces):
            e = jnp.exp2((scores[p] - m).astype(BF16))
            t = _dot(vts[p], e)
            acc = t if acc is None else acc + t
        l = acc[DV_A:DV_A + 1, :]
        ot = acc[:DV_A, :sub] * (1.0 / l[:, :sub]) - acc[:DV_A, sub:] * (lam / l[:, sub:])
        o = ot.T
        ms = jnp.mean(o * o, axis=-1, keepdims=True)
        o = o * lax.rsqrt(ms + EPS) * gd_ref[...] * (1.0 - lam_init)
        o_ref[r * sub:(r + 1) * sub, head_cols(hh)] = o.astype(o_ref.dtype)

    if cache_slot is not None:
        for c_ref, src in zip(refs[4 + 2 * n_pieces + n_alias:], kv_refs[:2]):
            dst = c_ref if n_alias else c_ref.at[cache_slot]
            for hh in range(hps):
                dst[hh] = src[:, head_cols(hh)]
            if not n_alias:
                for other in range(c_ref.shape[0]):
                    if other != cache_slot:
                        c_ref[other] = jnp.zeros(c_ref.shape[1:], c_ref.dtype)

    nsub = q_ref.shape[0] // sub
    units = [(hh, r) for hh in range(hps) for r in range(nsub)]
    nxt = qk(units[0])
    vts = None
    for n, u in enumerate(units):
        cur = nxt
        if u[1] == 0:
            vts = [v_aug(u[0], p) for p in range(n_pieces)]
        if n + 1 < len(units):
            nxt = qk(units[n + 1])
        finish(u, cur, vts)


def _attention(q_arr, pieces, v_transposed, head_major, lambda_qk_l, g_diff_l, n_seq, seqlen, tq, hps, q_prepped,
               lam_init, cache_out=None):
    per_seq = seqlen // tq
    width = hps * LANES
    in_specs = [
        pl.BlockSpec((4, DH_A), lambda b, h, i: (0, 0)),
        pl.BlockSpec((1, DV_A), lambda b, h, i: (0, 0)),
        pl.BlockSpec((tq, width), lambda b, h, i: (b * per_seq + i, h)),
    ]
    args = [lambda_qk_l, g_diff_l.reshape(1, DV_A), q_arr]
    for k_arr, v_arr, k_spec, v_spec in pieces:
        in_specs += [k_spec, v_spec]
        args += [k_arr, v_arr]
    out_specs = [pl.BlockSpec((tq, width), lambda b, h, i: (b * per_seq + i, h))]
    out_shape = [jax.ShapeDtypeStruct((n_seq * seqlen, W_A), BF16)]
    aliases = {}
    n_alias = 0
    if cache_out is not None:
        layer, caches = cache_out
        assert hps == H_A and per_seq == 1
        out_shape += [jax.ShapeDtypeStruct((n_seq, DEPTH, H_A, seqlen, LANES), F32)] * 2
        if caches is None:
            out_specs += [pl.BlockSpec((None, DEPTH, H_A, seqlen, LANES), lambda b, h, i: (b, 0, 0, 0, 0))] * 2
        else:
            out_specs += [pl.BlockSpec((None, None, H_A, seqlen, LANES), lambda b, h, i: (b, layer, 0, 0, 0))] * 2
            n_alias = 2
            aliases = {len(args): 1, len(args) + 1: 2}
            in_specs += [pl.BlockSpec(memory_space=pl.ANY)] * 2
            args += list(caches)
    return pl.pallas_call(
        functools.partial(_attn_kernel, n_pieces=len(pieces), n_alias=n_alias,
                          cache_slot=None if cache_out is None else cache_out[0],
                          q_prepped=q_prepped, v_transposed=v_transposed, head_major=head_major, lam_init=lam_init,
                          sub=128, hps=hps),
        grid=(n_seq, H_A // hps, per_seq),
        in_specs=in_specs,
        out_specs=out_specs,
        out_shape=out_shape,
        input_output_aliases=aliases,
        compiler_params=_cparams(("parallel", "parallel", "arbitrary"), 48),
        name="diff_attention",
    )(*args)


def _pool_kernel(u_ref, w_ref, sc_ref, o_ref, *, seqlen):
    rows = u_ref.shape[0]
    pos = lax.broadcasted_iota(jnp.int32, (rows, POOL_CH), 0) % seqlen
    for g, win in enumerate(POOL_WINDOWS):
        sl = slice(g * POOL_CH, (g + 1) * POOL_CH)
        u = u_ref[:, sl]
        half = win // 2
        acc = u
        for d in range(-half, half):
            if d == 0:
                continue
            shifted = pltpu.roll(u, (-d) % rows, 0)
            valid = (pos + d >= 0) & (pos + d < seqlen)
            acc = acc + jnp.where(valid, shifted, 0.0)
        cnt = (jnp.minimum(pos + half, seqlen) - jnp.maximum(pos - half, 0)).astype(F32)
        y = acc / cnt - u
        o_ref[:, sl] = (_dot(y.astype(BF16), w_ref[g].astype(BF16)) * sc_ref[:, sl]).astype(BF16)


def _pool(z, w_pool_l, pool_scale_l, seqlen, rows):
    t = z.shape[0]
    col_block = POOL_COL // W_C
    assert col_block * W_C == POOL_COL
    return pl.pallas_call(
        functools.partial(_pool_kernel, seqlen=seqlen),
        grid=(t // rows,),
        in_specs=[pl.BlockSpec((rows, W_C), lambda i: (i, col_block)),
                  pl.BlockSpec((N_POOL, POOL_CH, POOL_CH), lambda i: (0, 0, 0)),
                  pl.BlockSpec((1, W_C), lambda i: (0, 0))],
        out_specs=pl.BlockSpec((rows, W_C), lambda i: (i, 0)),
        out_shape=jax.ShapeDtypeStruct((t, W_C), BF16),
        compiler_params=_cparams(("parallel",), 48),
        name="pool_mixer",
    )(z, w_pool_l, pool_scale_l.reshape(1, W_C))


def _halo_specs(tm, width, col_block, t):
    nb = t // SUBLANES
    per = tm // SUBLANES
    prev = pl.BlockSpec((SUBLANES, width), lambda i, *_: (jnp.maximum(i * per - 1, 0), col_block))
    nxt = pl.BlockSpec((SUBLANES, width), lambda i, *_: (jnp.minimum((i + 1) * per, nb - 1), col_block))
    return prev, nxt


def _replace_rows(a, fixes):
    sub = lax.broadcasted_iota(jnp.int32, (SUBLANES, a.shape[1]), 0)
    parts, cur = [], 0
    for row, val in sorted(fixes, key=lambda f: f[0]):
        s0 = row // SUBLANES * SUBLANES
        if s0 > cur:
            parts.append(a[cur:s0])
        parts.append(jnp.where(sub == row % SUBLANES, val, a[s0:s0 + SUBLANES]))
        cur = s0 + SUBLANES
    if cur < a.shape[0]:
        parts.append(a[cur:])
    return jnp.concatenate(parts, axis=0)


def _dn_prep_kernel(x_ref, xprev_ref, xnext_ref, gate_ref, cw_ref, gp_ref, qkv_ref, gb_ref, *, seqlen, tm):
    i = pl.program_id(0)
    cw = cw_ref[...]
    cblk = 256
    starts = list(range(0, tm, seqlen)) if seqlen < tm else [0]
    ends = [r + seqlen - 1 for r in range(0, tm, seqlen)] if seqlen < tm else [tm - 1]
    prev_ok = (i * tm) % seqlen != 0 if seqlen > tm else False
    next_ok = ((i + 1) * tm) % seqlen != 0 if seqlen > tm else False
    zero_row = jnp.zeros((1, cblk), F32)
    for cb in range(QKVB_W // cblk):
        sl = slice(cb * cblk, (cb + 1) * cblk)
        x = x_ref[:, sl]
        prev_row = jnp.where(prev_ok, xprev_ref[SUBLANES - 1:SUBLANES, sl], zero_row)
        next_row = jnp.where(next_ok, xnext_ref[0:1, sl], zero_row)
        xm = _replace_rows(pltpu.roll(x, 1, 0), [(r, prev_row if r == 0 else zero_row) for r in starts])
        xp = _replace_rows(pltpu.roll(x, tm - 1, 0), [(r, next_row if r == tm - 1 else zero_row) for r in ends])
        y = xm * cw[0:1, sl] + x * cw[1:2, sl] + xp * cw[2:3, sl]
        y = y * (1.0 / (1.0 + jnp.exp2(y * -LOG2E)))
        for hh in range(cblk // LANES):
            c0 = cb * cblk + hh * LANES
            yh = y[:, hh * LANES:(hh + 1) * LANES]
            if c0 < 2 * H_B * DK_B:
                inv = lax.rsqrt(jnp.sum(yh * yh, axis=-1, keepdims=True) + EPS)
                yh = yh * (inv * (DK_B ** -0.5) if c0 < H_B * DK_B else inv)
            qkv_ref[:, c0:c0 + LANES] = yh
    lane = lax.broadcasted_iota(jnp.int32, (tm, LANES), 1)
    for d in range(2):
        raw = gate_ref[:, d * LANES:(d + 1) * LANES]
        gp = gp_ref[d]
        beta = 1.0 / (1.0 + jnp.exp(-raw))
        xa = raw + gp[1:2]
        softplus = jnp.maximum(xa, 0.0) + jnp.log1p(jnp.exp(-jnp.abs(xa)))
        gdec = -jnp.exp(gp[0:1]) * softplus
        gb_ref[d] = jnp.where(lane < H_B, beta, jnp.where(lane < 2 * H_B, gdec, 0.0))


def _dn_prep(z, conv_qkv_l, gate_params, seqlen, tm):
    t = z.shape[0]
    prev, nxt = _halo_specs(tm, QKVB_W, 1, t)
    return pl.pallas_call(
        functools.partial(_dn_prep_kernel, seqlen=seqlen, tm=tm),
        grid=(t // tm,),
        in_specs=[pl.BlockSpec((tm, QKVB_W), lambda i: (i, 1)), prev, nxt,
                  pl.BlockSpec((tm, 2 * LANES), lambda i: (i, GATE_COL // (2 * LANES))),
                  pl.BlockSpec((3, QKVB_W), lambda i: (0, 0)),
                  pl.BlockSpec((2, 2, LANES), lambda i: (0, 0, 0))],
        out_specs=[pl.BlockSpec((tm, QKVB_W), lambda i: (i, 0)),
                   pl.BlockSpec((2, tm, LANES), lambda i: (0, i, 0))],
        out_shape=[jax.ShapeDtypeStruct((t, QKVB_W), F32), jax.ShapeDtypeStruct((2, t, LANES), F32)],
        compiler_params=_cparams(("parallel",), 48),
        name="deltanet_prep",
    )(z, z, z, z, conv_qkv_l, gate_params)


def _dn_scan_kernel(*refs, cs, has_s0, state_slot):
    qkv_refs, gb_refs = refs[0:2], refs[2:4]
    s0_ref = refs[4] if has_s0 else None
    o_refs, sfin_ref, s_scr = refs[-4:-2], refs[-2], refs[-1]
    st = pl.program_id(1)

    @pl.when(st == 0)
    def _():
        if has_s0:
            s_scr[...] = s0_ref[...]
        else:
            s_scr[...] = jnp.zeros_like(s_scr)

    ii = lax.broadcasted_iota(jnp.int32, (CHUNK, CHUNK), 0)
    jj = lax.broadcasted_iota(jnp.int32, (CHUNK, CHUNK), 1)
    incl_d = (ii >= jj, ii <= jj)
    strict_d = (ii > jj, ii < jj)
    eye = (ii == jj).astype(F32)
    eq_masks = [((ii >> (3 + k)) == (jj >> (3 + k))).astype(F32) for k in range(3)] + [jnp.ones((CHUNK, CHUNK), F32)]
    offs = []
    st_q, st_k, st_v, st_beta, st_gc, st_gr, st_gt, st_d = [], [], [], [], [], [], [], []
    for d in range(2):
        tri = incl_d[d].astype(F32)
        for pos in range(cs):
            off = (pos if d == 0 else cs - 1 - pos) * CHUNK
            offs.append(off)
            gbc = gb_refs[d][off:off + CHUNK, :]
            gcol = jnp.dot(tri, gbc, preferred_element_type=F32, precision=lax.Precision.HIGHEST)
            grow = gcol.T
            gtot = jnp.sum(gbc, axis=0, keepdims=True)
            for h in range(H_B):
                st_d.append(d)
                st_q.append(qkv_refs[d][off:off + CHUNK, h * DK_B:(h + 1) * DK_B])
                st_k.append(qkv_refs[d][off:off + CHUNK, (H_B + h) * DK_B:(H_B + h + 1) * DK_B])
                st_v.append(qkv_refs[d][off:off + CHUNK, (2 * H_B + h) * DK_B:(2 * H_B + h + 1) * DK_B])
                st_beta.append(gbc[:, h:h + 1])
                st_gc.append(gcol[:, H_B + h:H_B + h + 1])
                st_gr.append(grow[H_B + h:H_B + h + 1, :])
                st_gt.append(gtot[:, H_B + h:H_B + h + 1])
    rng = range(2 * cs * H_B)
    st_eg = [jnp.exp(st_gc[i]) for i in rng]
    st_kq = [_dot_nt(jnp.concatenate([st_k[i], st_q[i]], axis=0).astype(BF16), st_k[i].astype(BF16)) for i in rng]
    st_lmat, st_a, st_x0 = [], [], []
    for i in rng:
        incl, strict = incl_d[st_d[i]], strict_d[st_d[i]]
        dec = jnp.where(incl, jnp.exp(jnp.where(incl, st_gc[i] - st_gr[i], 0.0)), 0.0)
        st_lmat.append(jnp.where(strict, st_kq[i][:CHUNK] * dec, 0.0) * st_beta[i])
        st_a.append((st_kq[i][CHUNK:] * dec).astype(BF16))
        st_x0.append(jnp.concatenate([st_v[i] * st_beta[i], st_k[i] * (st_beta[i] * st_eg[i])], axis=1))
    st_lbf = [st_lmat[i] * eq_masks[0] for i in rng]
    st_lb = [st_lbf[i].astype(BF16) for i in rng]
    st_p2f = [_dot(st_lb[i], st_lb[i]) for i in rng]
    st_p2 = [st_p2f[i].astype(BF16) for i in rng]
    st_p3 = [_dot(st_lb[i], st_p2[i]) for i in rng]
    st_p4 = [_dot(st_p2[i], st_p2[i]).astype(BF16) for i in rng]
    st_t = [eye - st_lbf[i] + st_p2f[i] - st_p3[i] for i in rng]
    st_t = [st_t[i] + _dot(st_t[i].astype(BF16), st_p4[i]) for i in rng]
    for lvl in range(3):
        cmask = eq_masks[lvl + 1] - eq_masks[lvl]
        st_t16 = [st_t[i].astype(BF16) for i in rng]
        st_m = [_dot((st_lmat[i] * cmask).astype(BF16), st_t16[i]).astype(BF16) for i in rng]
        st_t = [st_t[i] - _dot(st_t16[i], st_m[i]) for i in rng]
    st_x = [st_x0[i] + _dot((st_t[i] - eye).astype(BF16), st_x0[i].astype(BF16)) for i in rng]
    s_cur = [s_scr[d, h] for d in range(2) for h in range(H_B)]
    nch = range(2 * H_B)
    for pos in range(cs):
        idx = [(d * cs + pos) * H_B + h for d in range(2) for h in range(H_B)]
        wq = [jnp.concatenate([st_x[i][:, DV_B:], st_q[i] * st_eg[i]], axis=0).astype(BF16) for i in idx]
        r = [_dot(wq[c], s_cur[c].astype(BF16)) for c in nch]
        vn16 = [(st_x[i][:, :DV_B] - r[c][:CHUNK]).astype(BF16) for c, i in enumerate(idx)]
        o = [r[c][CHUNK:] + _dot(st_a[i], vn16[c]) for c, i in enumerate(idx)]
        kd16 = [(st_k[i] * jnp.exp(st_gt[i] - st_gc[i])).astype(BF16) for i in idx]
        s_cur = [s_cur[c] * jnp.exp(st_gt[i]) + _dot_tn(kd16[c], vn16[c]) for c, i in enumerate(idx)]
        for c in nch:
            d, h = divmod(c, H_B)
            off = offs[d * cs + pos]
            o_refs[d][off:off + CHUNK, h * DV_B:(h + 1) * DV_B] = o[c].astype(BF16)
    for c in nch:
        s_scr[c // H_B, c % H_B] = s_cur[c]

    @pl.when(st == pl.num_programs(1) - 1)
    def _():
        if state_slot is None:
            sfin_ref[...] = s_scr[...]
        else:
            for other in range(sfin_ref.shape[0]):
                sfin_ref[other] = s_scr[...] if other == state_slot else jnp.zeros_like(s_scr)


def _dn_scan(qkv, gb, s0, layer, n_seq, seqlen, cs, state_out=None):
    t = qkv.shape[0]
    rows = cs * CHUNK
    nsteps = seqlen // rows
    fwd = lambda b, s: b * nsteps + s
    bwd = lambda b, s: b * nsteps + nsteps - 1 - s
    in_specs = [pl.BlockSpec((rows, QKVB_W), lambda b, s: (fwd(b, s), 0)),
                pl.BlockSpec((rows, QKVB_W), lambda b, s: (bwd(b, s), 0)),
                pl.BlockSpec((None, rows, LANES), lambda b, s: (0, fwd(b, s), 0)),
                pl.BlockSpec((None, rows, LANES), lambda b, s: (1, bwd(b, s), 0))]
    args = [qkv, qkv, gb, gb]
    if s0 is not None:
        in_specs.append(pl.BlockSpec((None, None, 2, H_B, DK_B, DV_B), lambda b, s: (b, layer, 0, 0, 0, 0)))
        args.append(s0)
    o_shape = jax.ShapeDtypeStruct((t, W_B), BF16)
    aliases = {}
    state_slot = None
    if state_out is None:
        s_spec = pl.BlockSpec((None, 2, H_B, DK_B, DV_B), lambda b, s: (b, 0, 0, 0, 0))
        s_shape = jax.ShapeDtypeStruct((n_seq, 2, H_B, DK_B, DV_B), F32)
    else:
        s_layer, s_prev = state_out
        s_shape = jax.ShapeDtypeStruct((n_seq, DEPTH, 2, H_B, DK_B, DV_B), F32)
        if s_prev is None:
            state_slot = s_layer
            s_spec = pl.BlockSpec((None, DEPTH, 2, H_B, DK_B, DV_B), lambda b, s: (b, 0, 0, 0, 0, 0))
        else:
            s_spec = pl.BlockSpec((None, None, 2, H_B, DK_B, DV_B), lambda b, s: (b, s_layer, 0, 0, 0, 0))
            aliases = {len(args): 2}
            in_specs.append(pl.BlockSpec(memory_space=pl.ANY))
            args.append(s_prev)
    return pl.pallas_call(
        functools.partial(_dn_scan_kernel, cs=cs, has_s0=s0 is not None, state_slot=state_slot),
        grid=(n_seq, nsteps),
        in_specs=in_specs,
        out_specs=[pl.BlockSpec((rows, W_B), lambda b, s: (fwd(b, s), 0)),
                   pl.BlockSpec((rows, W_B), lambda b, s: (bwd(b, s), 0)),
                   s_spec],
        out_shape=[o_shape, o_shape, s_shape],
        input_output_aliases=aliases,
        scratch_shapes=[pltpu.VMEM((2, H_B, DK_B, DV_B), F32)],
        compiler_params=_cparams(("parallel", "arbitrary"), 40),
        name="deltanet_scan",
    )(*args)


def _out_proj_kernel(oa_ref, of_ref, ob_ref, z_ref, oc_ref, x_ref, gt_ref, g1_ref, gd_ref, w_ref, o_ref, mix_ref):
    mix_ref[:, 0:W_A] = oa_ref[...]
    gd = gd_ref[...]
    for h in range(H_B):
        sl = slice(h * DV_B, (h + 1) * DV_B)
        ob = of_ref[:, sl].astype(F32) + ob_ref[:, sl].astype(F32)
        ms = jnp.mean(ob * ob, axis=-1, keepdims=True)
        ob = ob * lax.rsqrt(ms + EPS) * gd * _silu(z_ref[:, sl])
        mix_ref[:, W_A + h * DV_B:W_A + (h + 1) * DV_B] = ob.astype(BF16)
    mix_ref[:, W_A + W_B:] = oc_ref[...]
    o_ref[...] = _dot(mix_ref[...], w_ref[...])
    _norm_residual(o_ref, x_ref, o_ref, gt_ref[...] * g1_ref[...], o_ref.shape[0])


def _out_proj(oa, o_f, o_b, z, oc, x, gains, mod, g_delta_l, w_out16, layer, row_of_tile, tm):
    t = x.shape[0]
    return pl.pallas_call(
        _out_proj_kernel,
        grid=(t // tm,),
        in_specs=[pl.BlockSpec((tm, W_A), lambda i: (i, 0)),
                  pl.BlockSpec((tm, W_B), lambda i: (i, 0)),
                  pl.BlockSpec((tm, W_B), lambda i: (i, 0)),
                  pl.BlockSpec((tm, SEG), lambda i: (i, ZB_SEG)),
                  pl.BlockSpec((tm, W_C), lambda i: (i, 0)),
                  pl.BlockSpec((tm, D_MODEL), lambda i: (i, 0)),
                  _mod_spec(2, row_of_tile),
                  _gain_spec(1),
                  pl.BlockSpec((1, DV_B), lambda i: (0, 0)),
                  pl.BlockSpec((None, D_MODEL, D_MODEL), lambda i: (layer, 0, 0))],
        out_specs=pl.BlockSpec((tm, D_MODEL), lambda i: (i, 0)),
        out_shape=jax.ShapeDtypeStruct((t, D_MODEL), F32),
        scratch_shapes=[pltpu.VMEM((tm, D_MODEL), BF16)],
        compiler_params=_cparams(("parallel",), 56),
        name="out_proj",
    )(oa, o_f, o_b, z, oc, x, mod, gains, g_delta_l.reshape(1, DV_B), w_out16)


def _ffn_kernel(x_ref, xprev_ref, xnext_ref, g2_ref, sc_ref, sh_ref, gt_ref, g3_ref,
                wg_ref, wu_ref, cg_ref, cu_ref, wd_ref, o_ref, h_ref, p_ref, acc_ref, *, seqlen, tm):
    i = pl.program_id(0)
    j = pl.program_id(1)
    hb = SUBLANES

    @pl.when(j == 0)
    def _():
        gs = g2_ref[...] * (1.0 + sc_ref[...])
        sh = sh_ref[...]
        norm = lambda v: _modulated_norm(v, gs, sh).astype(BF16)
        h_ref[0:hb, :] = norm(xprev_ref[...])
        for r0 in range(0, tm, NORM_ROWS):
            h_ref[hb + r0:hb + r0 + NORM_ROWS, :] = norm(x_ref[r0:r0 + NORM_ROWS, :])
        h_ref[hb + tm:, :] = norm(xnext_ref[...])
        acc_ref[...] = jnp.zeros_like(acc_ref)

    tf = wg_ref.shape[1]
    pos = (i * tm + lax.broadcasted_iota(jnp.int32, (tm, tf), 0)) % seqlen
    has_prev = pos != 0
    has_next = pos != seqlen - 1

    def conv_branch(w_ref, c_ref):
        p_ref[...] = _dot(h_ref[...], w_ref[...])
        cw = c_ref[...]
        return (jnp.where(has_prev, p_ref[hb - 1:hb - 1 + tm, :], 0.0) * cw[0:1]
                + p_ref[hb:hb + tm, :] * cw[1:2]
                + jnp.where(has_next, p_ref[hb + 1:hb + 1 + tm, :], 0.0) * cw[2:3])

    gate = conv_branch(wg_ref, cg_ref)
    up = conv_branch(wu_ref, cu_ref)
    acc_ref[...] += _dot((_silu(gate) * up).astype(BF16), wd_ref[...])

    @pl.when(j == pl.num_programs(1) - 1)
    def _():
        _norm_residual(acc_ref, x_ref, o_ref, gt_ref[...] * g3_ref[...], tm)


def _ffn(x, gains, mod, w_up16, conv_ffn, w_down16, layer, row_of_tile, seqlen, tm, tf):
    t = x.shape[0]
    nf = D_FF // tf
    prev, nxt = _halo_specs(tm, D_MODEL, 0, t)
    return pl.pallas_call(
        functools.partial(_ffn_kernel, seqlen=seqlen, tm=tm),
        grid=(t // tm, nf),
        in_specs=[pl.BlockSpec((tm, D_MODEL), lambda i, j: (i, 0), pipeline_mode=pl.Buffered(1)), prev, nxt,
                  _gain_spec(2), _mod_spec(4, row_of_tile), _mod_spec(3, row_of_tile), _mod_spec(5, row_of_tile),
                  _gain_spec(3),
                  pl.BlockSpec((None, D_MODEL, tf), lambda i, j: (layer, 0, j)),
                  pl.BlockSpec((None, D_MODEL, tf), lambda i, j: (layer, 0, nf + j)),
                  pl.BlockSpec((None, 3, tf), lambda i, j: (layer, 0, j)),
                  pl.BlockSpec((None, 3, tf), lambda i, j: (layer, 0, nf + j)),
                  pl.BlockSpec((None, tf, D_MODEL), lambda i, j: (layer, j, 0))],
        out_specs=pl.BlockSpec((tm, D_MODEL), lambda i, j: (i, 0)),
        out_shape=jax.ShapeDtypeStruct((t, D_MODEL), F32),
        scratch_shapes=[pltpu.VMEM((tm + 2 * SUBLANES, D_MODEL), BF16),
                        pltpu.VMEM((tm + 2 * SUBLANES, tf), F32),
                        pltpu.VMEM((tm, D_MODEL), F32)],
        compiler_params=_cparams(("parallel", "arbitrary"), 60),
        name="conv_ffn",
    )(x, x, x, gains, mod, mod, mod, gains, w_up16, w_up16, conv_ffn, conv_ffn, w_down16)


def _pack_w_tail(w_in_t):
    tail = w_in_t[:, N_MAIN_SEG * SEG:, :]
    beta = tail[:, :2 * H_B]
    alpha = tail[:, 2 * H_B:4 * H_B]
    pool = tail[:, 4 * H_B:]
    pad = jnp.zeros((DEPTH, LANES - 2 * H_B, D_MODEL), w_in_t.dtype)
    rows = []
    for d in range(2):
        rows += [beta[:, d * H_B:(d + 1) * H_B], alpha[:, d * H_B:(d + 1) * H_B], pad]
    rows.append(pool)
    return jnp.concatenate(rows, axis=1).astype(BF16)


def _gate_params(a_log_l, dt_bias_l):
    gp = jnp.zeros((2, 2, LANES), F32)
    gp = gp.at[:, 0, H_B:2 * H_B].set(a_log_l.astype(F32))
    return gp.at[:, 1, H_B:2 * H_B].set(dt_bias_l.astype(F32))


def _layer(x, mod_l, row_of_tile_fn, p, l, n_seq, seqlen, ctx, new_caches=None):
    t = x.shape[0]
    gains = p['norm_gains'][l].reshape(4, 1, D_MODEL)
    lam_init = 0.8 - 0.6 * math.exp(-0.3 * l)
    tm_big = 1024
    assert t % tm_big == 0 and (seqlen % tm_big == 0 or tm_big % seqlen == 0)
    z = _in_proj(x, gains, mod_l, p['w_in16'], p['w_tail'], l, row_of_tile_fn(tm_big), tm_big)

    if ctx is None:
        kv_spec = lambda seg: pl.BlockSpec((seqlen, W_A), lambda b, h, i: (b, seg))
        pieces = [(z, z, kv_spec(1), kv_spec(2))]
        kv_prev = None if new_caches is None else new_caches[:2]
        oa, k_new, v_new = _attention(z, pieces, (False,), (False,), p['lambda_qk'][l], p['g_diff'][l], n_seq, seqlen,
                                      seqlen, H_A, False, lam_init, cache_out=(l, kv_prev))
    else:
        cache_k, cache_v, state_delta = ctx
        past = cache_k.shape[2]
        q_r, k_r, v_t = _rope_prep(z, seqlen)
        hps = 2
        lat_spec = pl.BlockSpec((seqlen, hps * LANES), lambda b, h, i: (b, h))
        vt_spec = pl.BlockSpec((hps * LANES, seqlen), lambda b, h, i: (h, b))
        ck = cache_k.transpose(0, 1, 3, 2, 4)
        cv = cache_v.transpose(0, 1, 3, 2, 4)
        c_spec = pl.BlockSpec((None, None, hps, past, LANES), lambda b, h, i: (b, l, h, 0, 0))
        pieces = [(k_r, v_t, lat_spec, vt_spec), (ck, cv, c_spec, c_spec)]
        oa, = _attention(q_r, pieces, (True, False), (False, True), p['lambda_qk'][l], p['g_diff'][l], n_seq, seqlen,
                         1024, hps, True, lam_init)

    qkv, gb = _dn_prep(z, p['conv_qkv'][l], _gate_params(p['a_log'][l], p['dt_bias'][l]), seqlen, tm_big)
    s_prev = None if new_caches is None else new_caches[2]
    o_f, o_b, s_fin = _dn_scan(qkv, gb, None if ctx is None else ctx[2], l, n_seq, seqlen, 4,
                               state_out=(l, s_prev) if ctx is None else None)

    oc = _pool(z, p['w_pool'][l], p['pool_scale'][l], seqlen, max(seqlen, 1024))

    tm = 512
    x = _out_proj(oa, o_f, o_b, z, oc, x, gains, mod_l, p['g_delta'][l], p['w_out16'], l, row_of_tile_fn(tm), tm)
    tm = 1024
    x = _ffn(x, gains, mod_l, p['w_up16'], p['conv_ffn'], p['w_down16'], l, row_of_tile_fn(tm), seqlen, tm, 512)
    if ctx is None:
        return x, (k_new, v_new, s_fin)
    return x, None


def kernel(x_prompt, x_sample, cache_k, cache_v, state_delta, c, c_ctx, w_mod, b_mod, norm_gains, w_in, lambda_qk,
           g_diff, conv_qkv, a_log, dt_bias, g_delta, w_pool, pool_scale, w_out, w_up, conv_ffn, w_down):
    batch, seq, _ = x_prompt.shape
    dec_batch, dec_seq, _ = x_sample.shape
    w_in_t = w_in.transpose(0, 2, 1)
    p = {'norm_gains': norm_gains, 'lambda_qk': lambda_qk, 'g_diff': g_diff, 'conv_qkv': conv_qkv, 'a_log': a_log,
         'dt_bias': dt_bias, 'g_delta': g_delta, 'w_pool': w_pool, 'pool_scale': pool_scale, 'conv_ffn': conv_ffn,
         'w_in16': w_in_t.astype(BF16), 'w_tail': _pack_w_tail(w_in_t),
         'w_out16': w_out.astype(BF16), 'w_up16': w_up.astype(BF16), 'w_down16': w_down.astype(BF16)}

    assert 1 + dec_batch <= SUBLANES
    cond8 = jnp.concatenate([c_ctx[None, :], c, jnp.zeros((SUBLANES - 1 - dec_batch, D_MODEL), F32)], axis=0)
    mod = _modulation(cond8, w_mod, b_mod).reshape(DEPTH, SUBLANES, 6, 1, D_MODEL)

    ctx_rows = lambda tm: (lambda i: 0)
    lat_rows = lambda tm: (lambda i: 1 + (i * tm) // dec_seq)

    xp = x_prompt.reshape(batch * seq, D_MODEL)
    new_caches = None
    for l in range(DEPTH):
        xp, new_caches = _layer(xp, mod[l], ctx_rows, p, l, batch, seq, None, new_caches)
    xs = x_sample.reshape(dec_batch * dec_seq, D_MODEL)
    for l in range(DEPTH):
        xs, _ = _layer(xs, mod[l], lat_rows, p, l, dec_batch, dec_seq, (cache_k, cache_v, state_delta))
    return (xp.reshape(batch, seq, D_MODEL), xs.reshape(dec_batch, dec_seq, D_MODEL),
            new_caches[0].transpose(0, 1, 3, 2, 4), new_caches[1].transpose(0, 1, 3, 2, 4), new_caches[2])
```

```python
import functools
import math

import jax
import jax.numpy as jnp
from jax import lax
from jax.experimental import pallas as pl
from jax.experimental.pallas import tpu as pltpu

D_MODEL = 2048
DEPTH = 2
GRID_W = 64
H_A = 6
DH_A = 64
DV_A = 2 * DH_A
H_B = 6
DK_B = 128
DV_B = 128
CHUNK = 64
N_POOL = 4
POOL_CH = 128
POOL_WINDOWS = (2, 4, 8, 16)
W_A = H_A * DV_A
W_B = H_B * DV_B
W_C = N_POOL * POOL_CH
D_FF = 5632
ROPE_THETA = 10000.0
EPS = 1e-6
LOG2E = math.log2(math.e)
Q_SCALE = DH_A ** -0.5 * LOG2E

LANES = 128
SUBLANES = 8
SEG = 768
QKVB_W = H_B * (2 * DK_B + DV_B)
Z_W = 8 * SEG
N_MAIN_SEG = 7
ZB_SEG = 6
GATE_COL = N_MAIN_SEG * SEG
POOL_COL = GATE_COL + 2 * LANES
N_MOD = 6 * D_MODEL
VMEM_PHYS = 64 * 1024 * 1024

F32 = jnp.float32
BF16 = jnp.bfloat16


def _cparams(sem, vmem_mb):
    assert vmem_mb * 1024 * 1024 < VMEM_PHYS
    return pltpu.CompilerParams(dimension_semantics=sem, vmem_limit_bytes=vmem_mb * 1024 * 1024)


def _silu(x):
    return x * (1.0 / (1.0 + jnp.exp(-x)))


def _dot(a, b):
    return jnp.dot(a, b, preferred_element_type=F32)


def _dot_nt(a, b):
    return lax.dot_general(a, b, (((1,), (1,)), ((), ())), preferred_element_type=F32)


def _dot_tn(a, b):
    return lax.dot_general(a, b, (((0,), (0,)), ((), ())), preferred_element_type=F32)


def _mod_kernel(c_ref, w_ref, b_ref, o_ref):
    a = _silu(c_ref[...]).astype(BF16)
    o_ref[...] = _dot(a, w_ref[...].astype(BF16)) + b_ref[...]


def _modulation(cond8, w_mod, b_mod):
    tn = 1024
    return pl.pallas_call(
        _mod_kernel,
        grid=(DEPTH, N_MOD // tn),
        in_specs=[
            pl.BlockSpec((SUBLANES, D_MODEL), lambda l, j: (0, 0)),
            pl.BlockSpec((None, D_MODEL, tn), lambda l, j: (l, 0, j)),
            pl.BlockSpec((None, 1, tn), lambda l, j: (l, 0, j)),
        ],
        out_specs=pl.BlockSpec((None, SUBLANES, tn), lambda l, j: (l, 0, j)),
        out_shape=jax.ShapeDtypeStruct((DEPTH, SUBLANES, N_MOD), F32),
        compiler_params=_cparams(("parallel", "parallel"), 40),
        name="modulation",
    )(cond8, w_mod, b_mod.reshape(DEPTH, 1, N_MOD))


NORM_ROWS = 2 * SUBLANES


def _modulated_norm(x, gs, sh):
    ms = jnp.mean(x * x, axis=-1, keepdims=True)
    return x * lax.rsqrt(ms + EPS) * gs + sh


def _row_chunks(n_rows, body):
    for r0 in range(0, n_rows, NORM_ROWS):
        body(r0)


def _norm_residual(y_ref, x_ref, o_ref, gg, n_rows):
    def body(r0):
        rows = pl.ds(r0, NORM_ROWS)
        y = y_ref[rows, :]
        ms = jnp.mean(y * y, axis=-1, keepdims=True)
        o_ref[rows, :] = x_ref[rows, :] + y * lax.rsqrt(ms + EPS) * gg
    _row_chunks(n_rows, body)


def _in_proj_kernel(x_ref, g_ref, sc_ref, sh_ref, w_ref, wt_ref, o_ref, h_ref, *, tm):
    j = pl.program_id(1)

    @pl.when(j == 0)
    def _():
        gs = g_ref[...] * (1.0 + sc_ref[...])
        sh = sh_ref[...]

        def body(r0):
            rows = pl.ds(r0, NORM_ROWS)
            h_ref[rows, :] = _modulated_norm(x_ref[rows, :], gs, sh).astype(BF16)
        _row_chunks(tm, body)

    @pl.when(j < N_MAIN_SEG)
    def _():
        o_ref[...] = _dot_nt(h_ref[...], w_ref[...])

    @pl.when(j == N_MAIN_SEG)
    def _():
        o_ref[...] = _dot_nt(h_ref[...], wt_ref[...])


def _mod_spec(which, row_of_tile):
    return pl.BlockSpec((None, None, 1, D_MODEL), lambda i, *_: (row_of_tile(i), which, 0, 0))


def _gain_spec(k):
    return pl.BlockSpec((None, 1, D_MODEL), lambda i, *_: (k, 0, 0))


def _in_proj(x, gains, mod, w_in16, w_tail, layer, row_of_tile, tm):
    t = x.shape[0]
    return pl.pallas_call(
        functools.partial(_in_proj_kernel, tm=tm),
        grid=(t // tm, Z_W // SEG),
        in_specs=[
            pl.BlockSpec((tm, D_MODEL), lambda i, j: (i, 0)),
            _gain_spec(0),
            _mod_spec(1, row_of_tile),
            _mod_spec(0, row_of_tile),
            pl.BlockSpec((None, SEG, D_MODEL), lambda i, j: (layer, jnp.minimum(j, N_MAIN_SEG - 1), 0)),
            pl.BlockSpec((None, SEG, D_MODEL), lambda i, j: (layer, 0, 0)),
        ],
        out_specs=pl.BlockSpec((tm, SEG), lambda i, j: (i, j)),
        out_shape=jax.ShapeDtypeStruct((t, Z_W), F32),
        scratch_shapes=[pltpu.VMEM((tm, D_MODEL), BF16)],
        compiler_params=_cparams(("parallel", "arbitrary"), 48),
        name="in_proj",
    )(x, gains, mod, mod, w_in16, w_tail)


def _rope_tables(n):
    rows = n // GRID_W
    row = jnp.repeat(jnp.arange(rows), GRID_W)
    col = jnp.arange(rows * GRID_W) % GRID_W
    n_pair = DH_A // 4
    inv = ROPE_THETA ** (-jnp.arange(n_pair, dtype=F32) / n_pair)
    ang = jnp.concatenate([row[:, None] * inv, col[:, None] * inv], axis=-1)
    cos = jnp.repeat(jnp.cos(ang), 2, axis=-1)
    sin = jnp.repeat(jnp.sin(ang), 2, axis=-1) * jnp.tile(jnp.array([-1.0, 1.0], F32), DH_A // 2)
    return jnp.tile(cos, (1, 2)), jnp.tile(sin, (1, 2))


def _rope_kernel(q_ref, k_ref, v_ref, cos_ref, sin_ref, qo_ref, ko_ref, vo_ref):
    cos = cos_ref[...]
    sin = sin_ref[...]
    even = (lax.broadcasted_iota(jnp.int32, cos.shape, 1) % 2) == 0
    for h in range(H_A):
        sl = slice(h * LANES, (h + 1) * LANES)
        for src, dst, scale in ((q_ref, qo_ref, Q_SCALE), (k_ref, ko_ref, None)):
            x = src[:, sl]
            swapped = jnp.where(even, pltpu.roll(x, LANES - 1, 1), pltpu.roll(x, 1, 1))
            y = x * cos + swapped * sin
            if scale is not None:
                y = y * scale
            dst[:, sl] = y.astype(BF16)
        vo_ref[sl, :] = v_ref[:, sl].T.astype(BF16)


def _rope_prep(z, seqlen):
    t = z.shape[0]
    tm = 512
    cos, sin = _rope_tables(seqlen)
    per_seq = seqlen // tm
    tab_spec = pl.BlockSpec((tm, LANES), lambda i: (i % per_seq, 0))
    out = jax.ShapeDtypeStruct((t, SEG), BF16)
    return pl.pallas_call(
        _rope_kernel,
        grid=(t // tm,),
        in_specs=[pl.BlockSpec((tm, SEG), lambda i: (i, 0)),
                  pl.BlockSpec((tm, SEG), lambda i: (i, 1)),
                  pl.BlockSpec((tm, SEG), lambda i: (i, 2)),
                  tab_spec, tab_spec],
        out_specs=[pl.BlockSpec((tm, SEG), lambda i: (i, 0)), pl.BlockSpec((tm, SEG), lambda i: (i, 0)),
                   pl.BlockSpec((SEG, tm), lambda i: (0, i))],
        out_shape=[out, out, jax.ShapeDtypeStruct((SEG, t), BF16)],
        compiler_params=_cparams(("parallel",), 32),
        name="rope_prep",
    )(z, z, z, cos, sin)


def _attn_kernel(*refs, n_pieces, n_alias, cache_slot, q_prepped, v_transposed, head_major, lam_init, sub, hps):
    lq_ref, gd_ref, q_ref = refs[:3]
    kv_refs = refs[3:3 + 2 * n_pieces]
    o_ref = refs[3 + 2 * n_pieces + n_alias]
    lq = lq_ref[...]
    lam = (jnp.exp(jnp.sum(lq[0:1] * lq[1:2], axis=-1, keepdims=True))
           - jnp.exp(jnp.sum(lq[2:3] * lq[3:4], axis=-1, keepdims=True)) + lam_init)
    lo = lax.broadcasted_iota(jnp.int32, (sub, LANES), 1) < DH_A
    ones_rows = 2 * SUBLANES

    def head_cols(hh):
        return slice(hh * LANES, (hh + 1) * LANES)

    def keys(hh, p):
        k_ref = kv_refs[2 * p]
        return (k_ref[hh] if head_major[p] else k_ref[:, head_cols(hh)]).astype(BF16)

    def v_aug(hh, p):
        v_ref = kv_refs[2 * p + 1]
        if v_transposed[p]:
            vt = v_ref[head_cols(hh), :]
        else:
            vt = (v_ref[hh] if head_major[p] else v_ref[:, head_cols(hh)]).astype(F32).T
        vt = vt.astype(BF16)
        return jnp.concatenate([vt, jnp.ones((ones_rows, vt.shape[1]), BF16)], axis=0)

    def qk(u):
        hh, r = u
        q = q_ref[r * sub:(r + 1) * sub, head_cols(hh)]
        if not q_prepped:
            q = q * Q_SCALE
        zero = jnp.zeros_like(q)
        qq = jnp.concatenate([jnp.where(lo, q, zero), jnp.where(lo, zero, q)], axis=0).astype(BF16)
        return [_dot_nt(keys(hh, p), qq) for p in range(n_pieces)]

    def finish(u, scores, vts):
        hh, r = u
        m = scores[0].max(axis=0, keepdims=True)
        for s in scores[1:]:
            m = jnp.maximum(m, s.max(axis=0, keepdims=True))
        acc = None
        for p in range(n_pieces):
            e = jnp.exp2((scores[p] - m).astype(BF16))
            t = _dot(vts[p], e)
            acc = t if acc is None else acc + t
        l = acc[DV_A:DV_A + 1, :]
        ot = acc[:DV_A, :sub] * (1.0 / l[:, :sub]) - acc[:DV_A, sub:] * (lam / l[:, sub:])
        o = ot.T
        ms = jnp.mean(o * o, axis=-1, keepdims=True)
        o = o * lax.rsqrt(ms + EPS) * gd_ref[...] * (1.0 - lam_init)
        o_ref[r * sub:(r + 1) * sub, head_cols(hh)] = o.astype(o_ref.dtype)

    if cache_slot is not None:
        for c_ref, src in zip(refs[4 + 2 * n_pieces + n_alias:], kv_refs[:2]):
            dst = c_ref if n_alias else c_ref.at[cache_slot]
            for hh in range(hps):
                dst[hh] = src[:, head_cols(hh)]
            if not n_alias:
                for other in range(c_ref.shape[0]):
                    if other != cache_slot:
                        c_ref[other] = jnp.zeros(c_ref.shape[1:], c_ref.dtype)

    nsub = q_ref.shape[0] // sub
    units = [(hh, r) for hh in range(hps) for r in range(nsub)]
    nxt = qk(units[0])
    vts = None
    for n, u in enumerate(units):
        cur = nxt
        if u[1] == 0:
            vts = [v_aug(u[0], p) for p in range(n_pieces)]
        if n + 1 < len(units):
            nxt = qk(units[n + 1])
        finish(u, cur, vts)


def _attention(q_arr, pieces, v_transposed, head_major, lambda_qk_l, g_diff_l, n_seq, seqlen, tq, hps, q_prepped,
               lam_init, cache_out=None):
    per_seq = seqlen // tq
    width = hps * LANES
    in_specs = [
        pl.BlockSpec((4, DH_A), lambda b, h, i: (0, 0)),
        pl.BlockSpec((1, DV_A), lambda b, h, i: (0, 0)),
        pl.BlockSpec((tq, width), lambda b, h, i: (b * per_seq + i, h)),
    ]
    args = [lambda_qk_l, g_diff_l.reshape(1, DV_A), q_arr]
    for k_arr, v_arr, k_spec, v_spec in pieces:
        in_specs += [k_spec, v_spec]
        args += [k_arr, v_arr]
    out_specs = [pl.BlockSpec((tq, width), lambda b, h, i: (b * per_seq + i, h))]
    out_shape = [jax.ShapeDtypeStruct((n_seq * seqlen, W_A), BF16)]
    aliases = {}
    n_alias = 0
    if cache_out is not None:
        layer, caches = cache_out
        assert hps == H_A and per_seq == 1
        out_shape += [jax.ShapeDtypeStruct((n_seq, DEPTH, H_A, seqlen, LANES), F32)] * 2
        if caches is None:
            out_specs += [pl.BlockSpec((None, DEPTH, H_A, seqlen, LANES), lambda b, h, i: (b, 0, 0, 0, 0))] * 2
        else:
            out_specs += [pl.BlockSpec((None, None, H_A, seqlen, LANES), lambda b, h, i: (b, layer, 0, 0, 0))] * 2
            n_alias = 2
            aliases = {len(args): 1, len(args) + 1: 2}
            in_specs += [pl.BlockSpec(memory_space=pl.ANY)] * 2
            args += list(caches)
    return pl.pallas_call(
        functools.partial(_attn_kernel, n_pieces=len(pieces), n_alias=n_alias,
                          cache_slot=None if cache_out is None else cache_out[0],
                          q_prepped=q_prepped, v_transposed=v_transposed, head_major=head_major, lam_init=lam_init,
                          sub=128, hps=hps),
        grid=(n_seq, H_A // hps, per_seq),
        in_specs=in_specs,
        out_specs=out_specs,
        out_shape=out_shape,
        input_output_aliases=aliases,
        compiler_params=_cparams(("parallel", "parallel", "arbitrary"), 48),
        name="diff_attention",
    )(*args)


def _pool_kernel(u_ref, w_ref, sc_ref, o_ref, *, seqlen):
    rows = u_ref.shape[0]
    pos = lax.broadcasted_iota(jnp.int32, (rows, POOL_CH), 0) % seqlen
    for g, win in enumerate(POOL_WINDOWS):
        sl = slice(g * POOL_CH, (g + 1) * POOL_CH)
        u = u_ref[:, sl]
        half = win // 2
        acc = u
        for d in range(-half, half):
            if d == 0:
                continue
            shifted = pltpu.roll(u, (-d) % rows, 0)
            valid = (pos + d >= 0) & (pos + d < seqlen)
            acc = acc + jnp.where(valid, shifted, 0.0)
        cnt = (jnp.minimum(pos + half, seqlen) - jnp.maximum(pos - half, 0)).astype(F32)
        y = acc / cnt - u
        o_ref[:, sl] = (_dot(y.astype(BF16), w_ref[g].astype(BF16)) * sc_ref[:, sl]).astype(BF16)


def _pool(z, w_pool_l, pool_scale_l, seqlen, rows):
    t = z.shape[0]
    col_block = POOL_COL // W_C
    assert col_block * W_C == POOL_COL
    return pl.pallas_call(
        functools.partial(_pool_kernel, seqlen=seqlen),
        grid=(t // rows,),
        in_specs=[pl.BlockSpec((rows, W_C), lambda i: (i, col_block)),
                  pl.BlockSpec((N_POOL, POOL_CH, POOL_CH), lambda i: (0, 0, 0)),
                  pl.BlockSpec((1, W_C), lambda i: (0, 0))],
        out_specs=pl.BlockSpec((rows, W_C), lambda i: (i, 0)),
        out_shape=jax.ShapeDtypeStruct((t, W_C), BF16),
        compiler_params=_cparams(("parallel",), 48),
        name="pool_mixer",
    )(z, w_pool_l, pool_scale_l.reshape(1, W_C))


def _halo_specs(tm, width, col_block, t):
    nb = t // SUBLANES
    per = tm // SUBLANES
    prev = pl.BlockSpec((SUBLANES, width), lambda i, *_: (jnp.maximum(i * per - 1, 0), col_block))
    nxt = pl.BlockSpec((SUBLANES, width), lambda i, *_: (jnp.minimum((i + 1) * per, nb - 1), col_block))
    return prev, nxt


def _replace_rows(a, fixes):
    sub = lax.broadcasted_iota(jnp.int32, (SUBLANES, a.shape[1]), 0)
    parts, cur = [], 0
    for row, val in sorted(fixes, key=lambda f: f[0]):
        s0 = row // SUBLANES * SUBLANES
        if s0 > cur:
            parts.append(a[cur:s0])
        parts.append(jnp.where(sub == row % SUBLANES, val, a[s0:s0 + SUBLANES]))
        cur = s0 + SUBLANES
    if cur < a.shape[0]:
        parts.append(a[cur:])
    return jnp.concatenate(parts, axis=0)


def _dn_prep_kernel(x_ref, xprev_ref, xnext_ref, gate_ref, cw_ref, gp_ref, qkv_ref, gb_ref, *, seqlen, tm):
    i = pl.program_id(0)
    cw = cw_ref[...]
    cblk = 256
    starts = list(range(0, tm, seqlen)) if seqlen < tm else [0]
    ends = [r + seqlen - 1 for r in range(0, tm, seqlen)] if seqlen < tm else [tm - 1]
    prev_ok = (i * tm) % seqlen != 0 if seqlen > tm else False
    next_ok = ((i + 1) * tm) % seqlen != 0 if seqlen > tm else False
    zero_row = jnp.zeros((1, cblk), F32)
    for cb in range(QKVB_W // cblk):
        sl = slice(cb * cblk, (cb + 1) * cblk)
        x = x_ref[:, sl]
        prev_row = jnp.where(prev_ok, xprev_ref[SUBLANES - 1:SUBLANES, sl], zero_row)
        next_row = jnp.where(next_ok, xnext_ref[0:1, sl], zero_row)
        xm = _replace_rows(pltpu.roll(x, 1, 0), [(r, prev_row if r == 0 else zero_row) for r in starts])
        xp = _replace_rows(pltpu.roll(x, tm - 1, 0), [(r, next_row if r == tm - 1 else zero_row) for r in ends])
        y = xm * cw[0:1, sl] + x * cw[1:2, sl] + xp * cw[2:3, sl]
        y = y * (1.0 / (1.0 + jnp.exp2(y * -LOG2E)))
        for hh in range(cblk // LANES):
            c0 = cb * cblk + hh * LANES
            yh = y[:, hh * LANES:(hh + 1) * LANES]
            if c0 < 2 * H_B * DK_B:
                inv = lax.rsqrt(jnp.sum(yh * yh, axis=-1, keepdims=True) + EPS)
                yh = yh * (inv * (DK_B ** -0.5) if c0 < H_B * DK_B else inv)
            qkv_ref[:, c0:c0 + LANES] = yh
    lane = lax.broadcasted_iota(jnp.int32, (tm, LANES), 1)
    for d in range(2):
        raw = gate_ref[:, d * LANES:(d + 1) * LANES]
        gp = gp_ref[d]
        beta = 1.0 / (1.0 + jnp.exp(-raw))
        xa = raw + gp[1:2]
        softplus = jnp.maximum(xa, 0.0) + jnp.log1p(jnp.exp(-jnp.abs(xa)))
        gdec = -jnp.exp(gp[0:1]) * softplus
        gb_ref[d] = jnp.where(lane < H_B, beta, jnp.where(lane < 2 * H_B, gdec, 0.0))


def _dn_prep(z, conv_qkv_l, gate_params, seqlen, tm):
    t = z.shape[0]
    prev, nxt = _halo_specs(tm, QKVB_W, 1, t)
    return pl.pallas_call(
        functools.partial(_dn_prep_kernel, seqlen=seqlen, tm=tm),
        grid=(t // tm,),
        in_specs=[pl.BlockSpec((tm, QKVB_W), lambda i: (i, 1)), prev, nxt,
                  pl.BlockSpec((tm, 2 * LANES), lambda i: (i, GATE_COL // (2 * LANES))),
                  pl.BlockSpec((3, QKVB_W), lambda i: (0, 0)),
                  pl.BlockSpec((2, 2, LANES), lambda i: (0, 0, 0))],
        out_specs=[pl.BlockSpec((tm, QKVB_W), lambda i: (i, 0)),
                   pl.BlockSpec((2, tm, LANES), lambda i: (0, i, 0))],
        out_shape=[jax.ShapeDtypeStruct((t, QKVB_W), F32), jax.ShapeDtypeStruct((2, t, LANES), F32)],
        compiler_params=_cparams(("parallel",), 48),
        name="deltanet_prep",
    )(z, z, z, z, conv_qkv_l, gate_params)


def _dn_scan_kernel(*refs, cs, has_s0, state_slot):
    qkv_refs, gb_refs = refs[0:2], refs[2:4]
    s0_ref = refs[4] if has_s0 else None
    o_refs, sfin_ref, s_scr = refs[-4:-2], refs[-2], refs[-1]
    st = pl.program_id(1)

    @pl.when(st == 0)
    def _():
        if has_s0:
            s_scr[...] = s0_ref[...]
        else:
            s_scr[...] = jnp.zeros_like(s_scr)

    ii = lax.broadcasted_iota(jnp.int32, (CHUNK, CHUNK), 0)
    jj = lax.broadcasted_iota(jnp.int32, (CHUNK, CHUNK), 1)
    incl_d = (ii >= jj, ii <= jj)
    strict_d = (ii > jj, ii < jj)
    eye = (ii == jj).astype(F32)
    eq_masks = [((ii >> (3 + k)) == (jj >> (3 + k))).astype(F32) for k in range(3)] + [jnp.ones((CHUNK, CHUNK), F32)]
    offs = []
    st_q, st_k, st_v, st_beta, st_gc, st_gr, st_gt, st_d = [], [], [], [], [], [], [], []
    for d in range(2):
        tri = incl_d[d].astype(F32)
        for pos in range(cs):
            off = (pos if d == 0 else cs - 1 - pos) * CHUNK
            offs.append(off)
            gbc = gb_refs[d][off:off + CHUNK, :]
            gcol = jnp.dot(tri, gbc, preferred_element_type=F32, precision=lax.Precision.HIGHEST)
            grow = gcol.T
            gtot = jnp.sum(gbc, axis=0, keepdims=True)
            for h in range(H_B):
                st_d.append(d)
                st_q.append(qkv_refs[d][off:off + CHUNK, h * DK_B:(h + 1) * DK_B])
                st_k.append(qkv_refs[d][off:off + CHUNK, (H_B + h) * DK_B:(H_B + h + 1) * DK_B])
                st_v.append(qkv_refs[d][off:off + CHUNK, (2 * H_B + h) * DK_B:(2 * H_B + h + 1) * DK_B])
                st_beta.append(gbc[:, h:h + 1])
                st_gc.append(gcol[:, H_B + h:H_B + h + 1])
                st_gr.append(grow[H_B + h:H_B + h + 1, :])
                st_gt.append(gtot[:, H_B + h:H_B + h + 1])
    st_eg, st_a, st_x = {}, {}, {}

    def chains_of(positions):
        return [(d * cs + pos) * H_B + h for pos in positions for d in range(2) for h in range(H_B)]

    def inversion_stages(ids):
        kq, lmat, x0 = {}, {}, {}
        for i in ids:
            st_eg[i] = jnp.exp(st_gc[i])
            kq[i] = _dot_nt(jnp.concatenate([st_k[i], st_q[i]], axis=0).astype(BF16), st_k[i].astype(BF16))
        yield
        for i in ids:
            incl, strict = incl_d[st_d[i]], strict_d[st_d[i]]
            dec = jnp.where(incl, jnp.exp(jnp.where(incl, st_gc[i] - st_gr[i], 0.0)), 0.0)
            lmat[i] = jnp.where(strict, kq[i][:CHUNK] * dec, 0.0) * st_beta[i]
            st_a[i] = (kq[i][CHUNK:] * dec).astype(BF16)
            x0[i] = jnp.concatenate([st_v[i] * st_beta[i], st_k[i] * (st_beta[i] * st_eg[i])], axis=1)
        lbf = {i: lmat[i] * eq_masks[0] for i in ids}
        lb = {i: lbf[i].astype(BF16) for i in ids}
        p2f = {i: _dot(lb[i], lb[i]) for i in ids}
        yield
        p2 = {i: p2f[i].astype(BF16) for i in ids}
        p3 = {i: _dot(lb[i], p2[i]) for i in ids}
        p4 = {i: _dot(p2[i], p2[i]).astype(BF16) for i in ids}
        yield
        t = {i: eye - lbf[i] + p2f[i] - p3[i] for i in ids}
        t = {i: t[i] + _dot(t[i].astype(BF16), p4[i]) for i in ids}
        yield
        for lvl in range(3):
            cmask = eq_masks[lvl + 1] - eq_masks[lvl]
            t16 = {i: t[i].astype(BF16) for i in ids}
            m = {i: _dot((lmat[i] * cmask).astype(BF16), t16[i]).astype(BF16) for i in ids}
            yield
            t = {i: t[i] - _dot(t16[i], m[i]) for i in ids}
            yield
        for i in ids:
            st_x[i] = x0[i] + _dot((t[i] - eye).astype(BF16), x0[i].astype(BF16))
        yield

    s_cur = [s_scr[d, h] for d in range(2) for h in range(H_B)]
    nch = range(2 * H_B)

    def state_stages(pos):
        idx = [(d * cs + pos) * H_B + h for d in range(2) for h in range(H_B)]
        wq = [jnp.concatenate([st_x[i][:, DV_B:], st_q[i] * st_eg[i]], axis=0).astype(BF16) for i in idx]
        r = [_dot(wq[c], s_cur[c].astype(BF16)) for c in nch]
        yield
        vn16 = [(st_x[i][:, :DV_B] - r[c][:CHUNK]).astype(BF16) for c, i in enumerate(idx)]
        o = [r[c][CHUNK:] + _dot(st_a[i], vn16[c]) for c, i in enumerate(idx)]
        kd16 = [(st_k[i] * jnp.exp(st_gt[i] - st_gc[i])).astype(BF16) for i in idx]
        for c, i in enumerate(idx):
            s_cur[c] = s_cur[c] * jnp.exp(st_gt[i]) + _dot_tn(kd16[c], vn16[c])
        for c in nch:
            d, h = divmod(c, H_B)
            off = offs[d * cs + pos]
            o_refs[d][off:off + CHUNK, h * DV_B:(h + 1) * DV_B] = o[c].astype(BF16)
        yield

    half = cs // 2
    for _ in inversion_stages(chains_of(range(half))):
        pass
    early = (stage for pos in range(half) for stage in state_stages(pos))
    for n, _ in enumerate(inversion_stages(chains_of(range(half, cs)))):
        if n % 2 == 1:
            next(early, None)
    for _ in early:
        pass
    for pos in range(half, cs):
        for _ in state_stages(pos):
            pass
    for c in nch:
        s_scr[c // H_B, c % H_B] = s_cur[c]

    @pl.when(st == pl.num_programs(1) - 1)
    def _():
        if state_slot is None:
            sfin_ref[...] = s_scr[...]
        else:
            for other in range(sfin_ref.shape[0]):
                sfin_ref[other] = s_scr[...] if other == state_slot else jnp.zeros_like(s_scr)


def _dn_scan(qkv, gb, s0, layer, n_seq, seqlen, cs, state_out=None):
    t = qkv.shape[0]
    rows = cs * CHUNK
    nsteps = seqlen // rows
    fwd = lambda b, s: b * nsteps + s
    bwd = lambda b, s: b * nsteps + nsteps - 1 - s
    in_specs = [pl.BlockSpec((rows, QKVB_W), lambda b, s: (fwd(b, s), 0)),
                pl.BlockSpec((rows, QKVB_W), lambda b, s: (bwd(b, s), 0)),
                pl.BlockSpec((None, rows, LANES), lambda b, s: (0, fwd(b, s), 0)),
                pl.BlockSpec((None, rows, LANES), lambda b, s: (1, bwd(b, s), 0))]
    args = [qkv, qkv, gb, gb]
    if s0 is not None:
        in_specs.append(pl.BlockSpec((None, None, 2, H_B, DK_B, DV_B), lambda b, s: (b, layer, 0, 0, 0, 0)))
        args.append(s0)
    o_shape = jax.ShapeDtypeStruct((t, W_B), BF16)
    aliases = {}
    state_slot = None
    if state_out is None:
        s_spec = pl.BlockSpec((None, 2, H_B, DK_B, DV_B), lambda b, s: (b, 0, 0, 0, 0))
        s_shape = jax.ShapeDtypeStruct((n_seq, 2, H_B, DK_B, DV_B), F32)
    else:
        s_layer, s_prev = state_out
        s_shape = jax.ShapeDtypeStruct((n_seq, DEPTH, 2, H_B, DK_B, DV_B), F32)
        if s_prev is None:
            state_slot = s_layer
            s_spec = pl.BlockSpec((None, DEPTH, 2, H_B, DK_B, DV_B), lambda b, s: (b, 0, 0, 0, 0, 0))
        else:
            s_spec = pl.BlockSpec((None, None, 2, H_B, DK_B, DV_B), lambda b, s: (b, s_layer, 0, 0, 0, 0))
            aliases = {len(args): 2}
            in_specs.append(pl.BlockSpec(memory_space=pl.ANY))
            args.append(s_prev)
    return pl.pallas_call(
        functools.partial(_dn_scan_kernel, cs=cs, has_s0=s0 is not None, state_slot=state_slot),
        grid=(n_seq, nsteps),
        in_specs=in_specs,
        out_specs=[pl.BlockSpec((rows, W_B), lambda b, s: (fwd(b, s), 0)),
                   pl.BlockSpec((rows, W_B), lambda b, s: (bwd(b, s), 0)),
                   s_spec],
        out_shape=[o_shape, o_shape, s_shape],
        input_output_aliases=aliases,
        scratch_shapes=[pltpu.VMEM((2, H_B, DK_B, DV_B), F32)],
        compiler_params=_cparams(("parallel", "arbitrary"), 40),
        name="deltanet_scan",
    )(*args)


def _out_proj_kernel(oa_ref, of_ref, ob_ref, z_ref, oc_ref, x_ref, gt_ref, g1_ref, gd_ref, w_ref, o_ref, mix_ref):
    mix_ref[:, 0:W_A] = oa_ref[...]
    gd = gd_ref[...]
    for h in range(H_B):
        sl = slice(h * DV_B, (h + 1) * DV_B)
        ob = of_ref[:, sl].astype(F32) + ob_ref[:, sl].astype(F32)
        ms = jnp.mean(ob * ob, axis=-1, keepdims=True)
        ob = ob * lax.rsqrt(ms + EPS) * gd * _silu(z_ref[:, sl])
        mix_ref[:, W_A + h * DV_B:W_A + (h + 1) * DV_B] = ob.astype(BF16)
    mix_ref[:, W_A + W_B:] = oc_ref[...]
    o_ref[...] = _dot(mix_ref[...], w_ref[...])
    _norm_residual(o_ref, x_ref, o_ref, gt_ref[...] * g1_ref[...], o_ref.shape[0])


def _out_proj(oa, o_f, o_b, z, oc, x, gains, mod, g_delta_l, w_out16, layer, row_of_tile, tm):
    t = x.shape[0]
    return pl.pallas_call(
        _out_proj_kernel,
        grid=(t // tm,),
        in_specs=[pl.BlockSpec((tm, W_A), lambda i: (i, 0)),
                  pl.BlockSpec((tm, W_B), lambda i: (i, 0)),
                  pl.BlockSpec((tm, W_B), lambda i: (i, 0)),
                  pl.BlockSpec((tm, SEG), lambda i: (i, ZB_SEG)),
                  pl.BlockSpec((tm, W_C), lambda i: (i, 0)),
                  pl.BlockSpec((tm, D_MODEL), lambda i: (i, 0)),
                  _mod_spec(2, row_of_tile),
                  _gain_spec(1),
                  pl.BlockSpec((1, DV_B), lambda i: (0, 0)),
                  pl.BlockSpec((None, D_MODEL, D_MODEL), lambda i: (layer, 0, 0))],
        out_specs=pl.BlockSpec((tm, D_MODEL), lambda i: (i, 0)),
        out_shape=jax.ShapeDtypeStruct((t, D_MODEL), F32),
        scratch_shapes=[pltpu.VMEM((tm, D_MODEL), BF16)],
        compiler_params=_cparams(("parallel",), 56),
        name="out_proj",
    )(oa, o_f, o_b, z, oc, x, mod, gains, g_delta_l.reshape(1, DV_B), w_out16)


def _ffn_kernel(x_ref, xprev_ref, xnext_ref, g2_ref, sc_ref, sh_ref, gt_ref, g3_ref,
                wg_ref, wu_ref, cg_ref, cu_ref, wd_ref, o_ref, h_ref, p_ref, acc_ref, *, seqlen, tm):
    i = pl.program_id(0)
    j = pl.program_id(1)
    hb = SUBLANES

    @pl.when(j == 0)
    def _():
        gs = g2_ref[...] * (1.0 + sc_ref[...])
        sh = sh_ref[...]
        norm = lambda v: _modulated_norm(v, gs, sh).astype(BF16)
        h_ref[0:hb, :] = norm(xprev_ref[...])
        for r0 in range(0, tm, NORM_ROWS):
            h_ref[hb + r0:hb + r0 + NORM_ROWS, :] = norm(x_ref[r0:r0 + NORM_ROWS, :])
        h_ref[hb + tm:, :] = norm(xnext_ref[...])
        acc_ref[...] = jnp.zeros_like(acc_ref)

    tf = wg_ref.shape[1]
    pos = (i * tm + lax.broadcasted_iota(jnp.int32, (tm, tf), 0)) % seqlen
    has_prev = pos != 0
    has_next = pos != seqlen - 1

    def conv_branch(w_ref, c_ref):
        p_ref[...] = _dot(h_ref[...], w_ref[...])
        cw = c_ref[...]
        return (jnp.where(has_prev, p_ref[hb - 1:hb - 1 + tm, :], 0.0) * cw[0:1]
                + p_ref[hb:hb + tm, :] * cw[1:2]
                + jnp.where(has_next, p_ref[hb + 1:hb + 1 + tm, :], 0.0) * cw[2:3])

    gate = conv_branch(wg_ref, cg_ref)
    up = conv_branch(wu_ref, cu_ref)
    acc_ref[...] += _dot((_silu(gate) * up).astype(BF16), wd_ref[...])

    @pl.when(j == pl.num_programs(1) - 1)
    def _():
        _norm_residual(acc_ref, x_ref, o_ref, gt_ref[...] * g3_ref[...], tm)


def _ffn(x, gains, mod, w_up16, conv_ffn, w_down16, layer, row_of_tile, seqlen, tm, tf):
    t = x.shape[0]
    nf = D_FF // tf
    prev, nxt = _halo_specs(tm, D_MODEL, 0, t)
    return pl.pallas_call(
        functools.partial(_ffn_kernel, seqlen=seqlen, tm=tm),
        grid=(t // tm, nf),
        in_specs=[pl.BlockSpec((tm, D_MODEL), lambda i, j: (i, 0)), prev, nxt,
                  _gain_spec(2), _mod_spec(4, row_of_tile), _mod_spec(3, row_of_tile), _mod_spec(5, row_of_tile),
                  _gain_spec(3),
                  pl.BlockSpec((None, D_MODEL, tf), lambda i, j: (layer, 0, j)),
                  pl.BlockSpec((None, D_MODEL, tf), lambda i, j: (layer, 0, nf + j)),
                  pl.BlockSpec((None, 3, tf), lambda i, j: (layer, 0, j)),
                  pl.BlockSpec((None, 3, tf), lambda i, j: (layer, 0, nf + j)),
                  pl.BlockSpec((None, tf, D_MODEL), lambda i, j: (layer, j, 0))],
        out_specs=pl.BlockSpec((tm, D_MODEL), lambda i, j: (i, 0)),
        out_shape=jax.ShapeDtypeStruct((t, D_MODEL), F32),
        scratch_shapes=[pltpu.VMEM((tm + 2 * SUBLANES, D_MODEL), BF16),
                        pltpu.VMEM((tm + 2 * SUBLANES, tf), F32),
                        pltpu.VMEM((tm, D_MODEL), F32)],
        compiler_params=_cparams(("parallel", "arbitrary"), 56),
        name="conv_ffn",
    )(x, x, x, gains, mod, mod, mod, gains, w_up16, w_up16, conv_ffn, conv_ffn, w_down16)


def _pack_w_tail(w_in_t):
    tail = w_in_t[:, N_MAIN_SEG * SEG:, :]
    beta = tail[:, :2 * H_B]
    alpha = tail[:, 2 * H_B:4 * H_B]
    pool = tail[:, 4 * H_B:]
    pad = jnp.zeros((DEPTH, LANES - 2 * H_B, D_MODEL), w_in_t.dtype)
    rows = []
    for d in range(2):
        rows += [beta[:, d * H_B:(d + 1) * H_B], alpha[:, d * H_B:(d + 1) * H_B], pad]
    rows.append(pool)
    return jnp.concatenate(rows, axis=1).astype(BF16)


def _gate_params(a_log_l, dt_bias_l):
    gp = jnp.zeros((2, 2, LANES), F32)
    gp = gp.at[:, 0, H_B:2 * H_B].set(a_log_l.astype(F32))
    return gp.at[:, 1, H_B:2 * H_B].set(dt_bias_l.astype(F32))


def _layer(x, mod_l, row_of_tile_fn, p, l, n_seq, seqlen, ctx, new_caches=None):
    t = x.shape[0]
    gains = p['norm_gains'][l].reshape(4, 1, D_MODEL)
    lam_init = 0.8 - 0.6 * math.exp(-0.3 * l)
    tm_big = 1024
    assert t % tm_big == 0 and (seqlen % tm_big == 0 or tm_big % seqlen == 0)
    z = _in_proj(x, gains, mod_l, p['w_in16'], p['w_tail'], l, row_of_tile_fn(tm_big), tm_big)

    if ctx is None:
        kv_spec = lambda seg: pl.BlockSpec((seqlen, W_A), lambda b, h, i: (b, seg))
        pieces = [(z, z, kv_spec(1), kv_spec(2))]
        kv_prev = None if new_caches is None else new_caches[:2]
        oa, k_new, v_new = _attention(z, pieces, (False,), (False,), p['lambda_qk'][l], p['g_diff'][l], n_seq, seqlen,
                                      seqlen, H_A, False, lam_init, cache_out=(l, kv_prev))
    else:
        cache_k, cache_v, state_delta = ctx
        past = cache_k.shape[2]
        q_r, k_r, v_t = _rope_prep(z, seqlen)
        hps = 2
        lat_spec = pl.BlockSpec((seqlen, hps * LANES), lambda b, h, i: (b, h))
        vt_spec = pl.BlockSpec((hps * LANES, seqlen), lambda b, h, i: (h, b))
        ck = cache_k.transpose(0, 1, 3, 2, 4)
        cv = cache_v.transpose(0, 1, 3, 2, 4)
        c_spec = pl.BlockSpec((None, None, hps, past, LANES), lambda b, h, i: (b, l, h, 0, 0))
        pieces = [(k_r, v_t, lat_spec, vt_spec), (ck, cv, c_spec, c_spec)]
        oa, = _attention(q_r, pieces, (True, False), (False, True), p['lambda_qk'][l], p['g_diff'][l], n_seq, seqlen,
                         1024, hps, True, lam_init)

    qkv, gb = _dn_prep(z, p['conv_qkv'][l], _gate_params(p['a_log'][l], p['dt_bias'][l]), seqlen, tm_big)
    s_prev = None if new_caches is None else new_caches[2]
    o_f, o_b, s_fin = _dn_scan(qkv, gb, None if ctx is None else ctx[2], l, n_seq, seqlen, 4,
                               state_out=(l, s_prev) if ctx is None else None)

    oc = _pool(z, p['w_pool'][l], p['pool_scale'][l], seqlen, max(seqlen, 1024))

    tm = 512
    x = _out_proj(oa, o_f, o_b, z, oc, x, gains, mod_l, p['g_delta'][l], p['w_out16'], l, row_of_tile_fn(tm), tm)
    tm = 512
    x = _ffn(x, gains, mod_l, p['w_up16'], p['conv_ffn'], p['w_down16'], l, row_of_tile_fn(tm), seqlen, tm, 512)
    if ctx is None:
        return x, (k_new, v_new, s_fin)
    return x, None


def kernel(x_prompt, x_sample, cache_k, cache_v, state_delta, c, c_ctx, w_mod, b_mod, norm_gains, w_in, lambda_qk,
           g_diff, conv_qkv, a_log, dt_bias, g_delta, w_pool, pool_scale, w_out, w_up, conv_ffn, w_down):
    batch, seq, _ = x_prompt.shape
    dec_batch, dec_seq, _ = x_sample.shape
    w_in_t = w_in.transpose(0, 2, 1)
    p = {'norm_gains': norm_gains, 'lambda_qk': lambda_qk, 'g_diff': g_diff, 'conv_qkv': conv_qkv, 'a_log': a_log,
         'dt_bias': dt_bias, 'g_delta': g_delta, 'w_pool': w_pool, 'pool_scale': pool_scale, 'conv_ffn': conv_ffn,
         'w_in16': w_in_t.astype(BF16), 'w_tail': _pack_w_tail(w_in_t),
         'w_out16': w_out.astype(BF16), 'w_up16': w_up.astype(BF16), 'w_down16': w_down.astype(BF16)}

    assert 1 + dec_batch <= SUBLANES
    cond8 = jnp.concatenate([c_ctx[None, :], c, jnp.zeros((SUBLANES - 1 - dec_batch, D_MODEL), F32)], axis=0)
    mod = _modulation(cond8, w_mod, b_mod).reshape(DEPTH, SUBLANES, 6, 1, D_MODEL)

    ctx_rows = lambda tm: (lambda i: 0)
    lat_rows = lambda tm: (lambda i: 1 + (i * tm) // dec_seq)

    xp = x_prompt.reshape(batch * seq, D_MODEL)
    new_caches = None
    for l in range(DEPTH):
        xp, new_caches = _layer(xp, mod[l], ctx_rows, p, l, batch, seq, None, new_caches)
    xs = x_sample.reshape(dec_batch * dec_seq, D_MODEL)
    for l in range(DEPTH):
        xs, _ = _layer(xs, mod[l], lat_rows, p, l, dec_batch, dec_seq, (cache_k, cache_v, state_delta))
    return (xp.reshape(batch, seq, D_MODEL), xs.reshape(dec_batch, dec_seq, D_MODEL),
            new_caches[0].transpose(0, 1, 3, 2, 4), new_caches[1].transpose(0, 1, 3, 2, 4), new_caches[2])
```
